```python
import math
import jax
import jax.numpy as jnp
from jax import lax
import numpy as np

D_MODEL = 4096
BATCH = 4
SEQ = 2048
DEPTH = 2
DEC_BATCH = 8
DEC_SEQ = 4
PAST_LEN = 16384
PAGE_SIZE = 128

A_CHUNK = 128
A_GROUPS = 8
A_WIDTH = D_MODEL // 2
A_GROUP_DIM = A_WIDTH // A_GROUPS
B_HEADS = 16
B_HEAD_DIM = 128
B_WIDTH = B_HEADS * B_HEAD_DIM
B_PATTERNS = ((128, 1), (512, 4), (2048, 16))
B_WIN_MAX = max(w for w, _ in B_PATTERNS)
B_BLOCK = 128
C_HEADS = 8
C_QK_DIM = 128
C_V_DIM = 256
C_QK_WIDTH = C_HEADS * C_QK_DIM
C_V_WIDTH = C_HEADS * C_V_DIM
C_CHUNK = 64
F_BIAS_INIT = 3.0
N_BRANCH = 3
D_FF = 4 * D_MODEL
EPS = 1e-6
NEG_INF = -1e30

SPLIT_SIZES = (A_WIDTH, A_WIDTH, B_WIDTH, B_WIDTH, B_WIDTH, C_QK_WIDTH, C_QK_WIDTH,
               C_V_WIDTH, C_V_WIDTH, C_HEADS, C_HEADS, N_BRANCH * D_MODEL)
SPLIT_POINTS = tuple(int(v) for v in np.cumsum(SPLIT_SIZES)[:-1])
D_IN = sum(SPLIT_SIZES)

kernel_name = 'hybrid_gated_gmlp_dilswa_mlstm_step'


def _rms_norm(x, g):
    xf = x.astype(jnp.float32)
    y = xf * lax.rsqrt(jnp.mean(xf * xf, axis=-1, keepdims=True) + EPS)
    return (y * g.astype(jnp.float32)).astype(x.dtype)


def _chunk_spatial_gate(u, v, w_s, b_s):
    n, s = v.shape[:2]
    pad = (-s) % A_CHUNK
    nc = (s + pad) // A_CHUNK
    vc = jnp.pad(v, ((0, 0), (0, pad), (0, 0), (0, 0))).reshape(n, nc, A_CHUNK, A_GROUPS, A_GROUP_DIM)
    causal = jnp.tril(jnp.ones((A_CHUNK, A_CHUNK), dtype=bool))
    w = jnp.where(causal, w_s, 0.0)
    mixed = jnp.einsum('gts,ncsgd->nctgd', w, vc) + b_s.T[None, None, :, :, None]
    mixed = mixed.reshape(n, nc * A_CHUNK, A_GROUPS, A_GROUP_DIM)[:, :s]
    return u * mixed


def _masked_probs(scores, valid):
    sc = jnp.where(valid, scores, NEG_INF)
    m = jnp.max(sc, axis=-1)
    p = jnp.where(valid, jnp.exp(sc - m[..., None]), 0.0)
    return m, p


def _combine_by_denominator(ms, ss, nums):
    m = jnp.stack(ms)
    s = jnp.stack(ss)
    num = jnp.stack(nums)
    w = jax.nn.softmax(m + jnp.log(s), axis=0)
    return jnp.sum(w[..., None] * num / s[..., None], axis=0)


def _dilated_attn_prompt(q, k, v):
    n, s, h, dh = q.shape
    ms, ss, nums = [], [], []
    for win, dil in B_PATTERNS:
        span = win // dil
        ln = s // dil
        lp = -(-ln // B_BLOCK) * B_BLOCK
        nb = lp // B_BLOCK

        def strided(t):
            t = t.reshape(n, ln, dil, h, dh)
            t = jnp.pad(t, ((0, 0), (0, lp - ln), (0, 0), (0, 0), (0, 0)))
            return t.reshape(n, nb, B_BLOCK, dil, h, dh)

        qb, kb, vb = strided(q), strided(k), strided(v)
        pad_blk = ((0, 0), (1, 0), (0, 0), (0, 0), (0, 0), (0, 0))
        kc = jnp.concatenate([jnp.pad(kb, pad_blk)[:, :-1], kb], axis=2)
        vc = jnp.concatenate([jnp.pad(vb, pad_blk)[:, :-1], vb], axis=2)
        scores = jnp.einsum('nbqrhd,nbkrhd->nbrhqk', qb, kc, preferred_element_type=jnp.float32)
        qi = jnp.arange(B_BLOCK)[:, None]
        kj = jnp.arange(2 * B_BLOCK)[None, :]
        dist = qi + B_BLOCK - kj
        blk = jnp.arange(nb)[:, None, None]
        valid = (dist >= 0) & (dist <= span) & ((blk > 0) | (kj >= B_BLOCK))
        m, p = _masked_probs(scores, valid[None, :, None, None])
        den = jnp.sum(p, axis=-1)
        num = jnp.einsum('nbrhqk,nbkrhd->nbqrhd', p, vc.astype(jnp.float32))

        def unblock(t):
            t = t.reshape((n, lp, dil, h) + t.shape[5:])[:, :ln]
            return t.reshape((n, s, h) + t.shape[4:])

        ms.append(unblock(jnp.transpose(m, (0, 1, 4, 2, 3))))
        ss.append(unblock(jnp.transpose(den, (0, 1, 4, 2, 3))))
        nums.append(unblock(num))
    return _combine_by_denominator(ms, ss, nums)


def _dilated_attn_sample(q, k_all, v_all):
    t = q.shape[1]
    buf = k_all.shape[1] - t
    tq = jnp.arange(t)
    ms, ss, nums = [], [], []
    for win, dil in B_PATTERNS:
        j = jnp.arange(win // dil + 1)
        idx = buf + tq[:, None] - j[None, :] * dil
        valid = idx >= 0
        idx = jnp.maximum(idx, 0)
        kg = k_all[:, idx]
        vg = v_all[:, idx]
        scores = jnp.einsum('nthd,ntjhd->nhtj', q, kg, preferred_element_type=jnp.float32)
        m, p = _masked_probs(scores, valid[None, None])
        den = jnp.sum(p, axis=-1)
        num = jnp.einsum('nhtj,ntjhd->nthd', p, vg.astype(jnp.float32))
        ms.append(jnp.transpose(m, (0, 2, 1)))
        ss.append(jnp.transpose(den, (0, 2, 1)))
        nums.append(num)
    return _combine_by_denominator(ms, ss, nums)


def _mlstm(q, k, v, i_pre, f_pre, c0, n0, m0):
    n, s, h, dk = q.shape
    f32 = jnp.float32
    ln = math.gcd(s, C_CHUNK)
    nc = s // ln

    def chunks(t):
        t = t.astype(f32).reshape((n, nc, ln) + t.shape[2:])
        return jnp.swapaxes(jnp.moveaxis(t, 1, 0), 2, 3)

    qs = chunks(q)
    ks = chunks(k) * (dk ** -0.5)
    vs = chunks(v)
    ig = chunks(i_pre)
    lf = jax.nn.log_sigmoid(chunks(f_pre))
    causal = jnp.tril(jnp.ones((ln, ln), dtype=bool))

    def step(carry, xs):
        c, nv, m = carry
        qc, kc, vc, ic, lfc = xs
        b = jnp.cumsum(lfc, axis=-1)
        dmat = jnp.where(causal, b[..., :, None] - b[..., None, :] + ic[..., None, :], NEG_INF)
        inter = b + m[..., None]
        mt = jnp.maximum(inter, jnp.max(dmat, axis=-1))
        wmat = jnp.exp(dmat - mt[..., None]) * jnp.einsum('nhtk,nhsk->nhts', qc, kc)
        w_inter = jnp.exp(inter - mt)
        num = jnp.einsum('nhts,nhsv->nhtv', wmat, vc) + w_inter[..., None] * jnp.einsum('nhvk,nhtk->nhtv', c, qc)
        den = jnp.sum(wmat, axis=-1) + w_inter * jnp.einsum('nhk,nhtk->nht', nv, qc)
        hc = num / jnp.maximum(jnp.abs(den), jnp.exp(-mt))[..., None]
        m_new = mt[..., -1]
        w_end = jnp.exp(b[..., -1:] - b + ic - m_new[..., None])
        decay = jnp.exp(b[..., -1] + m - m_new)
        c_new = decay[..., None, None] * c + jnp.einsum('nhs,nhsv,nhsk->nhvk', w_end, vc, kc)
        n_new = decay[..., None] * nv + jnp.einsum('nhs,nhsk->nhk', w_end, kc)
        return (c_new, n_new, m_new), hc

    (cf, nf, mf), hs = lax.scan(step, (c0.astype(f32), n0.astype(f32), m0.astype(f32)), (qs, ks, vs, ig, lf))
    hs = jnp.moveaxis(jnp.swapaxes(hs, 2, 3), 0, 1).reshape(n, s, h, v.shape[-1])
    return hs, (cf, nf, mf)


def _layer(x, swa_past, mlstm0, norm1_g, w_in, ws_a, bs_a, norm_va_g, qn_g, kn_g, i_b, f_b, hn_c_g,
           w_branch_a, w_branch_b, w_branch_c, w_out, norm2_g, w_ff1, w_ff2):
    n, s, _ = x.shape
    proj = _rms_norm(x, norm1_g) @ w_in
    ua, va, qb, kb, vb, qc, kc, vc, oc, ic, fc, gates = jnp.split(proj, SPLIT_POINTS, axis=-1)
    ua = jax.nn.gelu(ua).reshape(n, s, A_GROUPS, A_GROUP_DIM)
    va = _rms_norm(jax.nn.gelu(va), norm_va_g)
    a_out = _chunk_spatial_gate(ua, va.reshape(n, s, A_GROUPS, A_GROUP_DIM), ws_a, bs_a).reshape(n, s, A_WIDTH)
    qh = _rms_norm(qb.reshape(n, s, B_HEADS, B_HEAD_DIM), qn_g) * (B_HEAD_DIM ** -0.5)
    kh = _rms_norm(kb.reshape(n, s, B_HEADS, B_HEAD_DIM), kn_g)
    vh = vb.reshape(n, s, B_HEADS, B_HEAD_DIM)
    if swa_past is None:
        o = _dilated_attn_prompt(qh, kh, vh)
        keep = min(B_WIN_MAX, s)
        kv_rows = (kh[:, s - keep:], vh[:, s - keep:])
    else:
        k_all = jnp.concatenate([swa_past[0], kh], axis=1)
        v_all = jnp.concatenate([swa_past[1], vh], axis=1)
        o = _dilated_attn_sample(qh, k_all, v_all)
        kv_rows = (kh, vh)
    b_out = o.astype(x.dtype).reshape(n, s, B_WIDTH)
    hc, mstate = _mlstm(qc.reshape(n, s, C_HEADS, C_QK_DIM), kc.reshape(n, s, C_HEADS, C_QK_DIM),
                        vc.reshape(n, s, C_HEADS, C_V_DIM), ic + i_b, fc + f_b, *mlstm0)
    hn = _rms_norm(hc.astype(x.dtype), hn_c_g.reshape(C_HEADS, C_V_DIM))
    c_out = (jax.nn.sigmoid(oc.reshape(n, s, C_HEADS, C_V_DIM)) * hn).reshape(n, s, C_V_WIDTH)
    g_a, g_b, g_c = jnp.split(jax.nn.sigmoid(gates), N_BRANCH, axis=-1)
    merged = g_a * (a_out @ w_branch_a) + g_b * (b_out @ w_branch_b) + g_c * (c_out @ w_branch_c)
    x = x + merged @ w_out
    ff = jax.nn.relu(_rms_norm(x, norm2_g) @ w_ff1)
    x = x + (ff * ff) @ w_ff2
    return x, kv_rows, mstate, va


def setup_inputs(seed: int = 0) -> dict:
    key = jax.random.key(seed)
    ks = jax.random.split(key, 26)
    swa_buf = min(B_WIN_MAX, PAST_LEN)

    def nrm(k, shape, scale):
        return jax.random.normal(k, shape, jnp.float32) * scale

    def gain(k, shape):
        return 1.0 + nrm(k, shape, 0.02)

    return {
        'x_prompt': nrm(ks[0], (BATCH, SEQ, D_MODEL), 1.0),
        'x_sample': nrm(ks[1], (DEC_BATCH, DEC_SEQ, D_MODEL), 1.0),
        'cache_swa_k': nrm(ks[2], (DEPTH, DEC_BATCH, swa_buf, B_HEADS, B_HEAD_DIM), 1.0),
        'cache_swa_v': nrm(ks[3], (DEPTH, DEC_BATCH, swa_buf, B_HEADS, B_HEAD_DIM), 1.0),
        'state_mlstm_C': nrm(ks[4], (DEPTH, DEC_BATCH, C_HEADS, C_V_DIM, C_QK_DIM), 0.1),
        'state_mlstm_n': nrm(ks[5], (DEPTH, DEC_BATCH, C_HEADS, C_QK_DIM), 0.1),
        'state_mlstm_m': nrm(ks[6], (DEPTH, DEC_BATCH, C_HEADS), 1.0),
        'norm1_g': gain(ks[7], (DEPTH, D_MODEL)),
        'w_in': nrm(ks[8], (DEPTH, D_MODEL, D_IN), D_MODEL ** -0.5),
        'ws_a': nrm(ks[9], (DEPTH, A_GROUPS, A_CHUNK, A_CHUNK), A_CHUNK ** -0.5),
        'bs_a': 1.0 + nrm(ks[10], (DEPTH, A_GROUPS, A_CHUNK), 0.1),
        'norm_va_g': gain(ks[11], (DEPTH, A_WIDTH)),
        'qn_g': gain(ks[12], (DEPTH, B_HEAD_DIM)),
        'kn_g': gain(ks[13], (DEPTH, B_HEAD_DIM)),
        'i_b': nrm(ks[14], (DEPTH, C_HEADS), 0.1),
        'f_b': F_BIAS_INIT + nrm(ks[15], (DEPTH, C_HEADS), 0.1),
        'hn_c_g': gain(ks[16], (DEPTH, C_V_WIDTH)),
        'w_branch_a': nrm(ks[17], (DEPTH, A_WIDTH, D_MODEL), A_WIDTH ** -0.5),
        'w_branch_b': nrm(ks[18], (DEPTH, B_WIDTH, D_MODEL), B_WIDTH ** -0.5),
        'w_branch_c': nrm(ks[19], (DEPTH, C_V_WIDTH, D_MODEL), C_V_WIDTH ** -0.5),
        'w_out': nrm(ks[20], (DEPTH, D_MODEL, D_MODEL), D_MODEL ** -0.5),
        'norm2_g': gain(ks[21], (DEPTH, D_MODEL)),
        'w_ff1': nrm(ks[22], (DEPTH, D_MODEL, D_FF), D_MODEL ** -0.5),
        'w_ff2': nrm(ks[23], (DEPTH, D_FF, D_MODEL), D_FF ** -0.5),
    }


def reference(x_prompt, x_sample, cache_swa_k, cache_swa_v, state_mlstm_C, state_mlstm_n, state_mlstm_m,
              norm1_g, w_in, ws_a, bs_a, norm_va_g, qn_g, kn_g, i_b, f_b, hn_c_g,
              w_branch_a, w_branch_b, w_branch_c, w_out, norm2_g, w_ff1, w_ff2):
    f32 = jnp.float32
    nbp = x_prompt.shape[0]
    xp, xs = x_prompt, x_sample
    kp_l, vp_l, cp_l, np_l, mp_l = [], [], [], [], []
    ks_l, vs_l, cs_l, ns_l, msl, va_l = [], [], [], [], [], []
    for l in range(DEPTH):
        lw = (norm1_g[l], w_in[l], ws_a[l], bs_a[l], norm_va_g[l], qn_g[l], kn_g[l], i_b[l], f_b[l],
              hn_c_g[l], w_branch_a[l], w_branch_b[l], w_branch_c[l], w_out[l], norm2_g[l], w_ff1[l], w_ff2[l])
        init = (jnp.zeros((nbp, C_HEADS, C_V_DIM, C_QK_DIM), f32), jnp.zeros((nbp, C_HEADS, C_QK_DIM), f32),
                jnp.zeros((nbp, C_HEADS), f32))
        xp, (kp, vp), (cp, n_p, mp), _ = _layer(xp, None, init, *lw)
        xs, (k_s, v_s), (c_s, n_s, m_s), va_s = _layer(
            xs, (cache_swa_k[l], cache_swa_v[l]), (state_mlstm_C[l], state_mlstm_n[l], state_mlstm_m[l]), *lw)
        kp_l.append(kp); vp_l.append(vp); cp_l.append(cp); np_l.append(n_p); mp_l.append(mp)
        ks_l.append(k_s); vs_l.append(v_s); cs_l.append(c_s); ns_l.append(n_s); msl.append(m_s); va_l.append(va_s)
    return (xp, xs,
            jnp.stack(kp_l), jnp.stack(vp_l), jnp.stack(cp_l), jnp.stack(np_l), jnp.stack(mp_l),
            jnp.stack(ks_l), jnp.stack(vs_l), jnp.stack(cs_l), jnp.stack(ns_l), jnp.stack(msl),
            jnp.stack(va_l))
```

```python
import functools
import math

import jax
import jax.numpy as jnp
from jax import lax
from jax.experimental import pallas as pl
from jax.experimental.pallas import tpu as pltpu

F32 = jnp.float32
BF16 = jnp.bfloat16

EPS = 1e-6
NEG_INF = -1e30

A_CHUNK = 128
A_GROUPS = 8
B_HEADS = 16
B_HEAD_DIM = 128
B_PATTERNS = ((128, 1), (512, 4), (2048, 16))
B_BLOCK = 128
C_HEADS = 8
C_QK_DIM = 128
C_V_DIM = 256
N_BRANCH = 3

V7X_VMEM_BYTES = 64 * 1024 * 1024
VMEM_LIMIT_BYTES = V7X_VMEM_BYTES - 8 * 1024 * 1024
LANES = 128


def _params(semantics):
    return pltpu.CompilerParams(dimension_semantics=semantics, vmem_limit_bytes=VMEM_LIMIT_BYTES)


def _gelu(x):
    c = math.sqrt(2.0 / math.pi)
    return 0.5 * x * (1.0 + jnp.tanh(c * (x + 0.044715 * (x * x * x))))


def _sigmoid(x):
    return 1.0 / (1.0 + jnp.exp(-x))


def _log_sigmoid(x):
    return -(jnp.maximum(-x, 0.0) + jnp.log(1.0 + jnp.exp(-jnp.abs(x))))


def _dot(a, b):
    return jnp.dot(a, b, preferred_element_type=F32)


def _dot_nt(a, b):
    return lax.dot_general(a, b, (((1,), (1,)), ((), ())), preferred_element_type=F32)


def _rmsnorm_kernel(x_ref, g_ref, o_ref):
    x = x_ref[...]
    y = x * lax.rsqrt(jnp.mean(x * x, axis=-1, keepdims=True) + EPS)
    o_ref[...] = (y * g_ref[...]).astype(o_ref.dtype)


def _rmsnorm_bf16(x, g):
    m, d = x.shape
    tm = min(m, 256)
    assert m % tm == 0
    return pl.pallas_call(
        _rmsnorm_kernel,
        grid=(m // tm,),
        in_specs=[pl.BlockSpec((tm, d), lambda i: (i, 0)), pl.BlockSpec((1, d), lambda i: (0, 0))],
        out_specs=pl.BlockSpec((tm, d), lambda i: (i, 0)),
        out_shape=jax.ShapeDtypeStruct((m, d), BF16),
        compiler_params=_params(("parallel",)),
    )(x, g.reshape(1, d))


def _mm_kernel(x_ref, w_ref, *rest, nk, epilogue):
    o_ref = rest[-1]
    extra = rest[:-1]
    part = _dot(x_ref[...], w_ref[...].astype(BF16))

    def finish(acc):
        if epilogue == "relu2":
            r = jnp.maximum(acc, 0.0)
            acc = r * r
        elif epilogue == "resid":
            acc = acc + extra[0][...]
        return acc.astype(o_ref.dtype)

    if nk == 1:
        o_ref[...] = finish(part)
    else:
        k = pl.program_id(2)

        @pl.when(k == 0)
        def _():
            o_ref[...] = part

        @pl.when(jnp.logical_and(k > 0, k < nk - 1))
        def _():
            o_ref[...] += part

        @pl.when(k == nk - 1)
        def _():
            o_ref[...] = finish(o_ref[...] + part)


def _matmul(x, w, *, n_cols=None, col_off=0, epilogue="none", resid=None, out_dtype=F32,
            tm=2048, tn=512, tk=None):
    m, kdim = x.shape
    n = w.shape[1] if n_cols is None else n_cols
    tm = min(tm, m)
    tn = min(tn, n)
    tk = kdim if tk is None else min(tk, kdim)
    assert m % tm == 0 and n % tn == 0 and kdim % tk == 0 and col_off % tn == 0
    nk = kdim // tk
    assert nk == 1 or out_dtype == F32
    joff = col_off // tn
    x_mode = pl.Buffered(1) if (nk == 1 and n // tn > 1 and tm * tk * 2 > (4 << 20)) else None
    in_specs = [pl.BlockSpec((tm, tk), lambda i, j, k: (i, k), pipeline_mode=x_mode),
                pl.BlockSpec((tk, tn), lambda i, j, k: (k, j + joff))]
    args = [x, w]
    if epilogue == "resid":
        in_specs.append(pl.BlockSpec((tm, tn), lambda i, j, k: (i, j)))
        args.append(resid)
    return pl.pallas_call(
        functools.partial(_mm_kernel, nk=nk, epilogue=epilogue),
        grid=(m // tm, n // tn, nk),
        in_specs=in_specs,
        out_specs=pl.BlockSpec((tm, tn), lambda i, j, k: (i, j)),
        out_shape=jax.ShapeDtypeStruct((m, n), out_dtype),
        compiler_params=_params(("parallel", "parallel", "arbitrary")),
    )(*args)


def _merge_kernel(a_ref, b_ref, c_ref, wa_ref, wb_ref, wc_ref, ga_ref, gb_ref, gc_ref, o_ref):
    acc = _sigmoid(ga_ref[...]) * _dot(a_ref[...], wa_ref[...].astype(BF16))
    acc += _sigmoid(gb_ref[...]) * _dot(b_ref[...], wb_ref[...].astype(BF16))
    acc += _sigmoid(gc_ref[...]) * _dot(c_ref[...], wc_ref[...].astype(BF16))
    o_ref[...] = acc.astype(o_ref.dtype)


def _merge(a, b, c, wa, wb, wc, gates, *, tm=1024, tn=256):
    m, kdim = a.shape
    n = wa.shape[1]
    tm = min(tm, m)
    assert m % tm == 0 and n % tn == 0 and gates.shape == (m, N_BRANCH * n)
    nj = n // tn
    big = tm * kdim * 2 > (4 << 20)

    def x_spec():
        return pl.BlockSpec((tm, kdim), lambda i, j: (i, 0), pipeline_mode=pl.Buffered(1) if big else None)

    def g_spec(branch):
        return pl.BlockSpec((tm, tn), lambda i, j: (i, j + branch * nj))

    w_spec = pl.BlockSpec((kdim, tn), lambda i, j: (0, j))
    return pl.pallas_call(
        _merge_kernel,
        grid=(m // tm, nj),
        in_specs=[x_spec(), x_spec(), x_spec(), w_spec, w_spec, w_spec, g_spec(0), g_spec(1), g_spec(2)],
        out_specs=pl.BlockSpec((tm, tn), lambda i, j: (i, j)),
        out_shape=jax.ShapeDtypeStruct((m, n), BF16),
        compiler_params=_params(("parallel", "parallel")),
    )(a, b, c, wa, wb, wc, gates, gates, gates)


def _gmlp_kernel(u_ref, v_ref, w_ref, b_ref, g_ref, a_ref, *va_refs, chunks, gdim):
    row = lax.broadcasted_iota(jnp.int32, (A_CHUNK, A_CHUNK), 0)
    col = lax.broadcasted_iota(jnp.int32, (A_CHUNK, A_CHUNK), 1)
    causal = col <= row
    w_groups = [jnp.where(causal, w_ref[g], 0.0).astype(BF16) for g in range(A_GROUPS)]
    gain = g_ref[...]
    bias = b_ref[...]
    for c in range(chunks):
        rows = pl.ds(c * A_CHUNK, A_CHUNK)
        v = _gelu(v_ref[0, rows, :])
        vn = v * lax.rsqrt(jnp.mean(v * v, axis=-1, keepdims=True) + EPS) * gain
        if va_refs:
            va_refs[0][0, rows, :] = vn
        vb = vn.astype(BF16)
        for g in range(A_GROUPS):
            cols = slice(g * gdim, (g + 1) * gdim)
            mixed = _dot(w_groups[g], vb[:, cols]) + bias[:, g:g + 1]
            u = _gelu(u_ref[0, rows, cols])
            a_ref[0, rows, cols] = (u * mixed).astype(a_ref.dtype)


def _gmlp(proj, ws, bs, norm_g, *, width, want_va, chunks=2):
    n, s, _ = proj.shape
    assert s % A_CHUNK == 0
    chunks = math.gcd(chunks, s // A_CHUNK)
    rows = chunks * A_CHUNK
    gdim = width // A_GROUPS
    out_shape = [jax.ShapeDtypeStruct((n, s, width), BF16)]
    out_specs = [pl.BlockSpec((1, rows, width), lambda i, j: (i, j, 0))]
    if want_va:
        out_shape.append(jax.ShapeDtypeStruct((n, s, width), F32))
        out_specs.append(pl.BlockSpec((1, rows, width), lambda i, j: (i, j, 0)))
    res = pl.pallas_call(
        functools.partial(_gmlp_kernel, chunks=chunks, gdim=gdim),
        grid=(n, s // rows),
        in_specs=[pl.BlockSpec((1, rows, width), lambda i, j: (i, j, 0)),
                  pl.BlockSpec((1, rows, width), lambda i, j: (i, j, 1)),
                  pl.BlockSpec((A_GROUPS, A_CHUNK, A_CHUNK), lambda i, j: (0, 0, 0)),
                  pl.BlockSpec((A_CHUNK, A_GROUPS), lambda i, j: (0, 0)),
                  pl.BlockSpec((1, width), lambda i, j: (0, 0))],
        out_specs=out_specs,
        out_shape=out_shape,
        compiler_params=_params(("parallel", "parallel")),
    )(proj, proj, ws, bs.T, norm_g.reshape(1, width))
    return res if want_va else (res[0], None)


def _qk_norm(x, g, scale):
    return x * lax.rsqrt(jnp.mean(x * x, axis=-1, keepdims=True) + EPS) * (g * scale)


def _attn_prompt_kernel(q_ref, k_ref, v_ref, qg_ref, kg_ref, o_ref, ko_ref, vo_ref, qs, ks, os_, ls, *, seq):
    qs[...] = _qk_norm(q_ref[0], qg_ref[...], B_HEAD_DIM ** -0.5)
    kn = _qk_norm(k_ref[0], kg_ref[...], 1.0)
    ks[...] = kn
    ko_ref[0] = kn
    vo_ref[0] = v_ref[0]

    qi = lax.broadcasted_iota(jnp.int32, (B_BLOCK, B_BLOCK), 0)
    kj = lax.broadcasted_iota(jnp.int32, (B_BLOCK, B_BLOCK), 1)
    mask_cur = kj <= qi
    mask_prev = kj >= qi

    def block(p, dil, start, with_prev):
        rows = pl.ds(start, B_BLOCK, stride=dil) if dil > 1 else pl.ds(start, B_BLOCK)
        qb = qs[rows, :].astype(BF16)
        s_cur = jnp.where(mask_cur, _dot_nt(qb, ks[rows, :].astype(BF16)), NEG_INF)
        m = jnp.max(s_cur, axis=-1, keepdims=True)
        if with_prev:
            pstart = start - B_BLOCK * dil
            prows = pl.ds(pstart, B_BLOCK, stride=dil) if dil > 1 else pl.ds(pstart, B_BLOCK)
            s_prev = jnp.where(mask_prev, _dot_nt(qb, ks[prows, :].astype(BF16)), NEG_INF)
            m = jnp.maximum(m, jnp.max(s_prev, axis=-1, keepdims=True))
        p_cur = jnp.exp(s_cur - m)
        den = jnp.sum(p_cur, axis=-1, keepdims=True)
        num = _dot(p_cur.astype(BF16), v_ref[0, rows, :].astype(BF16))
        if with_prev:
            p_prev = jnp.exp(s_prev - m)
            den = den + jnp.sum(p_prev, axis=-1, keepdims=True)
            num = num + _dot(p_prev.astype(BF16), v_ref[0, prows, :].astype(BF16))
        os_[p, rows, :] = num * (1.0 / den)
        ls[p, rows, :] = jnp.broadcast_to(m + jnp.log(den), (B_BLOCK, B_HEAD_DIM))

    for p, (win, dil) in enumerate(B_PATTERNS):
        nb = seq // (dil * B_BLOCK)

        def first(r, carry, p=p, dil=dil):
            block(p, dil, r, False)
            return carry

        lax.fori_loop(0, dil, first, 0)
        if nb > 1:
            def later(t, carry, p=p, dil=dil, nb=nb):
                r = t // (nb - 1)
                b = t % (nb - 1) + 1
                block(p, dil, b * (B_BLOCK * dil) + r, True)
                return carry

            lax.fori_loop(0, dil * (nb - 1), later, 0)

    l0, l1, l2 = ls[0], ls[1], ls[2]
    mx = jnp.maximum(jnp.maximum(l0, l1), l2)
    e0, e1, e2 = jnp.exp(l0 - mx), jnp.exp(l1 - mx), jnp.exp(l2 - mx)
    out = (e0 * os_[0] + e1 * os_[1] + e2 * os_[2]) * (1.0 / (e0 + e1 + e2))
    o_ref[0] = out.astype(o_ref.dtype)


def _attn_prompt(proj, qg, kg, *, col0):
    n, s, _ = proj.shape
    for _, dil in B_PATTERNS:
        assert s % (dil * B_BLOCK) == 0
    width = B_HEADS * B_HEAD_DIM
    blk = (1, s, B_HEAD_DIM)

    def in_spec(part):
        return pl.BlockSpec(blk, lambda i, h: (i, 0, col0 + part * B_HEADS + h))

    g_spec = pl.BlockSpec((1, B_HEAD_DIM), lambda i, h: (0, 0))
    out_spec = pl.BlockSpec(blk, lambda i, h: (i, 0, h))
    return pl.pallas_call(
        functools.partial(_attn_prompt_kernel, seq=s),
        grid=(n, B_HEADS),
        in_specs=[in_spec(0), in_spec(1), in_spec(2), g_spec, g_spec],
        out_specs=[out_spec, out_spec, out_spec],
        out_shape=[jax.ShapeDtypeStruct((n, s, width), BF16),
                   jax.ShapeDtypeStruct((n, s, width), F32),
                   jax.ShapeDtypeStruct((n, s, width), F32)],
        scratch_shapes=[pltpu.VMEM((s, B_HEAD_DIM), F32), pltpu.VMEM((s, B_HEAD_DIM), F32),
                        pltpu.VMEM((len(B_PATTERNS), s, B_HEAD_DIM), F32),
                        pltpu.VMEM((len(B_PATTERNS), s, B_HEAD_DIM), F32)],
        compiler_params=_params(("parallel", "parallel")),
    )(proj, proj, proj, qg.reshape(1, -1), kg.reshape(1, -1))


SAMPLE_ROWS = 8


def _attn_sample_kernel(q_ref, k_ref, v_ref, kc_ref, vc_ref, qg_ref, kg_ref, o_ref, ko_ref, vo_ref,
                        q8, k8, v8, *, t_new, buf):
    q8[...] = jnp.zeros_like(q8)
    k8[...] = jnp.zeros_like(k8)
    v8[...] = jnp.zeros_like(v8)
    kn = _qk_norm(k_ref[0], kg_ref[...], 1.0)
    ko_ref[0] = kn
    vo_ref[0] = v_ref[0]
    q8[0:t_new, :] = _qk_norm(q_ref[0], qg_ref[...], B_HEAD_DIM ** -0.5)
    k8[0:t_new, :] = kn
    v8[0:t_new, :] = v_ref[0]

    qb = q8[...].astype(BF16)
    s_c = _dot_nt(qb, kc_ref[0].astype(BF16))
    s_n = _dot_nt(qb, k8[...].astype(BF16))
    vcb = vc_ref[0].astype(BF16)
    vnew = v8[...].astype(BF16).astype(F32)
    t_c = lax.broadcasted_iota(jnp.int32, (SAMPLE_ROWS, buf), 0)
    r_c = lax.broadcasted_iota(jnp.int32, (SAMPLE_ROWS, buf), 1)
    dist_c = buf + t_c - r_c
    t_n = lax.broadcasted_iota(jnp.int32, (SAMPLE_ROWS, SAMPLE_ROWS), 0)
    u_n = lax.broadcasted_iota(jnp.int32, (SAMPLE_ROWS, SAMPLE_ROWS), 1)
    dist_n = t_n - u_n

    outs, lses = [], []
    for win, dil in B_PATTERNS:
        ok_c = jnp.logical_and((dist_c & (dil - 1)) == 0, dist_c <= win)
        ok_n = jnp.logical_and(jnp.logical_and(dist_n >= 0, (dist_n & (dil - 1)) == 0), u_n < t_new)
        sc = jnp.where(ok_c, s_c, NEG_INF)
        sn = jnp.where(ok_n, s_n, NEG_INF)
        m = jnp.maximum(jnp.max(sc, axis=-1, keepdims=True), jnp.max(sn, axis=-1, keepdims=True))
        pc = jnp.exp(sc - m)
        pn = jnp.exp(sn - m)
        den = jnp.sum(pc, axis=-1, keepdims=True) + jnp.sum(pn, axis=-1, keepdims=True)
        num = _dot(pc.astype(BF16), vcb)
        pn = pn.astype(BF16).astype(F32)
        for u in range(t_new):
            num = num + pn[:, u:u + 1] * vnew[u:u + 1, :]
        outs.append(num * (1.0 / den))
        lses.append(m + jnp.log(den))
    mx = jnp.maximum(jnp.maximum(lses[0], lses[1]), lses[2])
    es = [jnp.exp(l - mx) for l in lses]
    out = (es[0] * outs[0] + es[1] * outs[1] + es[2] * outs[2]) * (1.0 / (es[0] + es[1] + es[2]))
    o_ref[0] = out[0:t_new, :].astype(o_ref.dtype)


def _attn_sample(proj, cache_k, cache_v, qg, kg, *, col0):
    n, t_new, _ = proj.shape
    buf = cache_k.shape[1]
    assert t_new <= SAMPLE_ROWS and all(win <= buf for win, _ in B_PATTERNS)
    width = B_HEADS * B_HEAD_DIM
    blk = (1, t_new, B_HEAD_DIM)

    def in_spec(part):
        return pl.BlockSpec(blk, lambda i, h: (i, 0, col0 + part * B_HEADS + h))

    c_spec = pl.BlockSpec((1, buf, B_HEAD_DIM), lambda i, h: (i, 0, h))
    g_spec = pl.BlockSpec((1, B_HEAD_DIM), lambda i, h: (0, 0))
    out_spec = pl.BlockSpec(blk, lambda i, h: (i, 0, h))
    return pl.pallas_call(
        functools.partial(_attn_sample_kernel, t_new=t_new, buf=buf),
        grid=(n, B_HEADS),
        in_specs=[in_spec(0), in_spec(1), in_spec(2), c_spec, c_spec, g_spec, g_spec],
        out_specs=[out_spec, out_spec, out_spec],
        out_shape=[jax.ShapeDtypeStruct((n, t_new, width), F32)] * 3,
        scratch_shapes=[pltpu.VMEM((SAMPLE_ROWS, B_HEAD_DIM), F32)] * 3,
        compiler_params=_params(("parallel", "parallel")),
    )(proj, proj, proj, cache_k, cache_v, qg.reshape(1, -1), kg.reshape(1, -1))


def _split3(x):
    hi = x.astype(BF16)
    r1 = x - hi.astype(F32)
    mid = r1.astype(BF16)
    lo = (r1 - mid.astype(F32)).astype(BF16)
    return hi, mid, lo


def _mlstm_kernel(q_ref, k_ref, v_ref, o_ref, ifc_ref, ifr_ref, bc_ref, br_ref, hg_ref, c0_ref, n0_ref, m0_ref,
                  h_ref, cf_ref, nf_ref, mf_ref, ct_s, n_s, m_s, *, ln):
    j = pl.program_id(1)
    last = pl.num_programs(1) - 1

    @pl.when(j == 0)
    def _():
        for h in range(C_HEADS):
            ct_s[h] = c0_ref[0, h].T
        n_s[...] = n0_ref[0]
        m_s[...] = m0_ref[0]

    ifc = ifc_ref[0] + bc_ref[...]
    ifr = ifr_ref[0] + br_ref[...]
    row = lax.broadcasted_iota(jnp.int32, (ln, ln), 0)
    col = lax.broadcasted_iota(jnp.int32, (ln, ln), 1)
    causal = col <= row
    tri = jnp.where(causal, 1.0, 0.0).astype(BF16)
    triu = jnp.where(row <= col, 1.0, 0.0).astype(BF16)
    b_col = sum(_dot(tri, part) for part in _split3(_log_sigmoid(ifc)))
    b_row = sum(_dot(part, triu) for part in _split3(_log_sigmoid(ifr)))

    for h in range(C_HEADS):
        qc = q_ref[0, :, h * C_QK_DIM:(h + 1) * C_QK_DIM]
        qb = qc.astype(BF16)
        kc = k_ref[0, :, h * C_QK_DIM:(h + 1) * C_QK_DIM] * (C_QK_DIM ** -0.5)
        kb = kc.astype(BF16)
        vc = v_ref[0, :, h * C_V_DIM:(h + 1) * C_V_DIM]
        bcol = b_col[:, C_HEADS + h:C_HEADS + h + 1]
        brow = b_row[C_HEADS + h:C_HEADS + h + 1, :]
        icol = ifc[:, h:h + 1]
        irow = ifr[h:h + 1, :]
        m_prev = m_s[h:h + 1, 0:1]
        n_prev = n_s[h:h + 1, :]
        ct_prev = ct_s[h]

        dmat = jnp.where(causal, bcol - brow + irow, NEG_INF)
        inter = bcol + m_prev
        mt = jnp.maximum(inter, jnp.max(dmat, axis=-1, keepdims=True))
        wmat = jnp.exp(dmat - mt) * _dot_nt(qb, kb)
        w_inter = jnp.exp(inter - mt)
        num = _dot(wmat.astype(BF16), vc.astype(BF16)) + w_inter * _dot(qb, ct_prev.astype(BF16))
        den = jnp.sum(wmat, axis=-1, keepdims=True) + w_inter * jnp.sum(qc * n_prev, axis=-1, keepdims=True)
        hc = num * (1.0 / jnp.maximum(jnp.abs(den), jnp.exp(-mt)))

        m_new = mt[ln - 1:ln, :]
        b_last = bcol[ln - 1:ln, :]
        w_end = jnp.exp(b_last - bcol + icol - m_new)
        decay = jnp.exp(b_last + m_prev - m_new)
        ct_s[h] = decay * ct_prev + _dot(kc.T.astype(BF16), (w_end * vc).astype(BF16))
        n_s[h:h + 1, :] = decay * n_prev + jnp.sum(w_end * kc, axis=0, keepdims=True)
        m_s[h:h + 1, :] = jnp.broadcast_to(m_new, (1, LANES))

        gain = hg_ref[:, h * C_V_DIM:(h + 1) * C_V_DIM]
        hn = hc * lax.rsqrt(jnp.mean(hc * hc, axis=-1, keepdims=True) + EPS) * gain
        og = _sigmoid(o_ref[0, :, h * C_V_DIM:(h + 1) * C_V_DIM])
        h_ref[0, :, h * C_V_DIM:(h + 1) * C_V_DIM] = (og * hn).astype(h_ref.dtype)

    @pl.when(j == last)
    def _():
        for h in range(C_HEADS):
            cf_ref[0, h] = ct_s[h].T
        nf_ref[0] = n_s[...]
        mf_ref[0] = m_s[...]


def _mlstm(proj, ifc, ifr, bias_c, bias_r, hn_g, c0, n0, m0, *, col_q, col_k, col_v, col_o, ln):
    n, s, _ = proj.shape
    ln = min(ln, s)
    assert s % ln == 0 and ln % LANES == 0
    qk_w = C_HEADS * C_QK_DIM
    v_w = C_HEADS * C_V_DIM
    assert col_q % qk_w == 0 and col_k % qk_w == 0 and col_v % v_w == 0 and col_o % v_w == 0
    state_specs = [pl.BlockSpec((1, C_HEADS, C_V_DIM, C_QK_DIM), lambda i, j: (i, 0, 0, 0)),
                   pl.BlockSpec((1, C_HEADS, C_QK_DIM), lambda i, j: (i, 0, 0)),
                   pl.BlockSpec((1, C_HEADS, LANES), lambda i, j: (i, 0, 0))]
    return pl.pallas_call(
        functools.partial(_mlstm_kernel, ln=ln),
        grid=(n, s // ln),
        in_specs=[pl.BlockSpec((1, ln, qk_w), lambda i, j: (i, j, col_q // qk_w)),
                  pl.BlockSpec((1, ln, qk_w), lambda i, j: (i, j, col_k // qk_w)),
                  pl.BlockSpec((1, ln, v_w), lambda i, j: (i, j, col_v // v_w)),
                  pl.BlockSpec((1, ln, v_w), lambda i, j: (i, j, col_o // v_w)),
                  pl.BlockSpec((1, ln, LANES), lambda i, j: (i, j, 0)),
                  pl.BlockSpec((1, 2 * C_HEADS, ln), lambda i, j: (i, 0, j)),
                  pl.BlockSpec((1, LANES), lambda i, j: (0, 0)),
                  pl.BlockSpec((2 * C_HEADS, 1), lambda i, j: (0, 0)),
                  pl.BlockSpec((1, v_w), lambda i, j: (0, 0))] + state_specs,
        out_specs=[pl.BlockSpec((1, ln, v_w), lambda i, j: (i, j, 0))] + state_specs,
        out_shape=[jax.ShapeDtypeStruct((n, s, v_w), BF16),
                   jax.ShapeDtypeStruct((n, C_HEADS, C_V_DIM, C_QK_DIM), F32),
                   jax.ShapeDtypeStruct((n, C_HEADS, C_QK_DIM), F32),
                   jax.ShapeDtypeStruct((n, C_HEADS, LANES), F32)],
        scratch_shapes=[pltpu.VMEM((C_HEADS, C_QK_DIM, C_V_DIM), F32),
                        pltpu.VMEM((C_HEADS, C_QK_DIM), F32),
                        pltpu.VMEM((C_HEADS, LANES), F32)],
        compiler_params=_params(("parallel", "arbitrary")),
    )(proj, proj, proj, proj, ifc, ifr, bias_c, bias_r, hn_g.reshape(1, v_w), c0, n0, m0)


MLSTM_CHUNK = 256
GATE_PAD = 1e30


def _layer(x, swa_past, mlstm0, norm1_g, w_in, w_if, w_gates, ws_a, bs_a, norm_va_g, qn_g, kn_g, bias_c, bias_r,
           hn_c_g, w_branch_a, w_branch_b, w_branch_c, w_out, norm2_g, w_ff1, w_ff2):
    n, s, d = x.shape
    m = n * s
    a_width = d // 2
    n_main = w_in.shape[1] - w_if_cols() - N_BRANCH * d
    x2 = x.reshape(m, d)
    xn = _rmsnorm_bf16(x2, norm1_g)
    proj = _matmul(xn, w_in, n_cols=n_main)
    ifp = _matmul(xn, w_if)
    gates = _matmul(xn, w_gates)
    proj3 = proj.reshape(n, s, n_main)
    col_b = 2 * a_width
    col_c = col_b + 3 * B_HEADS * B_HEAD_DIM
    qk_w = C_HEADS * C_QK_DIM
    v_w = C_HEADS * C_V_DIM
    cols = dict(col_q=col_c, col_k=col_c + qk_w, col_v=col_c + 2 * qk_w, col_o=col_c + 2 * qk_w + v_w)
    c0, n0, m0 = mlstm0
    m0b = jnp.broadcast_to(m0[..., None], m0.shape + (LANES,))

    if swa_past is None:
        a_out, va = _gmlp(proj3, ws_a, bs_a, norm_va_g, width=a_width, want_va=False)
        b_out, k_rows, v_rows = _attn_prompt(proj3, qn_g, kn_g, col0=col_b // B_HEAD_DIM)
        ifc = ifp.reshape(n, s, LANES)
        ifr = jnp.swapaxes(ifc[:, :, :2 * C_HEADS], 1, 2)
        c_out, cf, nf, mf = _mlstm(proj3, ifc, ifr, bias_c, bias_r, hn_c_g, c0, n0, m0b, ln=MLSTM_CHUNK, **cols)
    else:
        pad = A_CHUNK - s
        assert pad >= 0
        proj_p = jnp.pad(proj3, ((0, 0), (0, pad), (0, 0)))
        a_out, va = _gmlp(proj_p, ws_a, bs_a, norm_va_g, width=a_width, want_va=True)
        a_out, va = a_out[:, :s], va[:, :s]
        b_out, k_rows, v_rows = _attn_sample(proj3, swa_past[0], swa_past[1], qn_g, kn_g, col0=col_b // B_HEAD_DIM)
        b_out = b_out.astype(BF16)
        lane = jnp.arange(LANES)
        pad_row = jnp.where(lane < C_HEADS, -GATE_PAD, jnp.where(lane < 2 * C_HEADS, GATE_PAD, 0.0)).astype(F32)
        ifc = jnp.concatenate([ifp.reshape(n, s, LANES), jnp.broadcast_to(pad_row, (n, pad, LANES))], axis=1)
        ifr = jnp.swapaxes(ifc[:, :, :2 * C_HEADS], 1, 2)
        c_out, cf, nf, mf = _mlstm(proj_p, ifc, ifr, bias_c, bias_r, hn_c_g, c0, n0, m0b, ln=A_CHUNK, **cols)
        c_out = c_out[:, :s]

    merged = _merge(a_out.reshape(m, -1), b_out.reshape(m, -1), c_out.reshape(m, -1),
                    w_branch_a, w_branch_b, w_branch_c, gates)
    x2 = _matmul(merged, w_out, epilogue="resid", resid=x2, tm=1024)
    h = _matmul(_rmsnorm_bf16(x2, norm2_g), w_ff1, epilogue="relu2", out_dtype=BF16)
    x2 = _matmul(h, w_ff2, epilogue="resid", resid=x2, tk=2048)
    return x2.reshape(n, s, d), (k_rows, v_rows), (cf, nf, mf[..., 0]), va


def w_if_cols():
    return 2 * C_HEADS


def kernel(x_prompt, x_sample, cache_swa_k, cache_swa_v, state_mlstm_C, state_mlstm_n, state_mlstm_m, norm1_g, w_in, ws_a, bs_a, norm_va_g, qn_g, kn_g, i_b, f_b, hn_c_g, w_branch_a, w_branch_b, w_branch_c, w_out, norm2_g, w_ff1, w_ff2):
    depth = w_in.shape[0]
    nbp, seq, d = x_prompt.shape
    nbs, t_new, _ = x_sample.shape
    buf = cache_swa_k.shape[2]
    hd = B_HEADS * B_HEAD_DIM
    n_gate = N_BRANCH * d
    n_main = w_in.shape[2] - n_gate - w_if_cols()
    xp, xs = x_prompt, x_sample
    outs = [[] for _ in range(11)]
    for l in range(depth):
        w_if = jnp.pad(w_in[l, :, n_main:n_main + w_if_cols()], ((0, 0), (0, LANES - w_if_cols())))
        w_gates = w_in[l, :, n_main + w_if_cols():]
        bias = jnp.concatenate([i_b[l], f_b[l]])
        bias_c = jnp.pad(bias, (0, LANES - 2 * C_HEADS)).reshape(1, LANES)
        bias_r = bias.reshape(2 * C_HEADS, 1)
        lw = (norm1_g[l], w_in[l], w_if, w_gates, ws_a[l], bs_a[l], norm_va_g[l], qn_g[l], kn_g[l], bias_c, bias_r,
              hn_c_g[l], w_branch_a[l], w_branch_b[l], w_branch_c[l], w_out[l], norm2_g[l], w_ff1[l], w_ff2[l])
        init = (jnp.zeros((nbp, C_HEADS, C_V_DIM, C_QK_DIM), F32), jnp.zeros((nbp, C_HEADS, C_QK_DIM), F32),
                jnp.zeros((nbp, C_HEADS), F32))
        xp, (kp, vp), (cp, n_p, mp), _ = _layer(xp, None, init, *lw)
        past = (cache_swa_k[l].reshape(nbs, buf, hd), cache_swa_v[l].reshape(nbs, buf, hd))
        xs, (k_s, v_s), (c_s, n_s, m_s), va_s = _layer(
            xs, past, (state_mlstm_C[l], state_mlstm_n[l], state_mlstm_m[l]), *lw)
        keep = min(max(w for w, _ in B_PATTERNS), seq)
        vals = (kp[:, seq - keep:].reshape(nbp, keep, B_HEADS, B_HEAD_DIM),
                vp[:, seq - keep:].reshape(nbp, keep, B_HEADS, B_HEAD_DIM), cp, n_p, mp,
                k_s.reshape(nbs, t_new, B_HEADS, B_HEAD_DIM), v_s.reshape(nbs, t_new, B_HEADS, B_HEAD_DIM),
                c_s, n_s, m_s, va_s)
        for o, v in zip(outs, vals):
            o.append(v)
    return (xp, xs) + tuple(jnp.stack(o) for o in outs)
```

```python
import functools
import math

import jax
import jax.numpy as jnp
from jax import lax
from jax.experimental import pallas as pl
from jax.experimental.pallas import tpu as pltpu

F32 = jnp.float32
BF16 = jnp.bfloat16

EPS = 1e-6
NEG_INF = -1e30

A_CHUNK = 128
A_GROUPS = 8
B_HEADS = 16
B_HEAD_DIM = 128
B_PATTERNS = ((128, 1), (512, 4), (2048, 16))
B_BLOCK = 128
C_HEADS = 8
C_QK_DIM = 128
C_V_DIM = 256
N_BRANCH = 3
IF_COLS = 2 * C_HEADS

V7X_VMEM_BYTES = 64 * 1024 * 1024
VMEM_LIMIT_BYTES = V7X_VMEM_BYTES - 8 * 1024 * 1024
LANES = 128
BF16_SUBLANES = 16
SINGLE_BUFFER_BYTES = 4 << 20


def _params(semantics):
    return pltpu.CompilerParams(dimension_semantics=semantics, vmem_limit_bytes=VMEM_LIMIT_BYTES)


def _gelu(x):
    c = math.sqrt(2.0 / math.pi)
    return 0.5 * x * (1.0 + jnp.tanh(c * (x + 0.044715 * (x * x * x))))


def _sigmoid(x):
    return 1.0 / (1.0 + jnp.exp(-x))


def _log_sigmoid(x):
    return -(jnp.maximum(-x, 0.0) + jnp.log(1.0 + jnp.exp(-jnp.abs(x))))


def _dot(a, b):
    return jnp.dot(a, b, preferred_element_type=F32)


def _dot_nt(a, b):
    return lax.dot_general(a, b, (((1,), (1,)), ((), ())), preferred_element_type=F32)


def _rmsnorm_kernel(x_ref, g_ref, o_ref):
    x = x_ref[...]
    y = x * lax.rsqrt(jnp.mean(x * x, axis=-1, keepdims=True) + EPS)
    o_ref[...] = (y * g_ref[...]).astype(o_ref.dtype)


def _rmsnorm_bf16(x, g, *, tm):
    m, d = x.shape
    assert m % tm == 0
    return pl.pallas_call(
        _rmsnorm_kernel,
        grid=(m // tm,),
        in_specs=[pl.BlockSpec((tm, d), lambda i: (i, 0)), pl.BlockSpec((1, d), lambda i: (0, 0))],
        out_specs=pl.BlockSpec((tm, d), lambda i: (i, 0)),
        out_shape=jax.ShapeDtypeStruct((m, d), BF16),
        compiler_params=_params(("parallel",)),
    )(x, g.reshape(1, d))


def _mm_kernel(x_ref, w_ref, *rest, nk, epilogue):
    o_ref = rest[-1]
    extra = rest[:-1]
    part = _dot(x_ref[...], w_ref[...].astype(BF16))

    def finish(acc):
        if epilogue == "relu2":
            r = jnp.maximum(acc, 0.0)
            acc = r * r
        elif epilogue == "resid":
            acc = acc + extra[0][...]
        return acc.astype(o_ref.dtype)

    if nk == 1:
        o_ref[...] = finish(part)
    else:
        k = pl.program_id(2)

        @pl.when(k == 0)
        def _():
            o_ref[...] = part

        @pl.when(jnp.logical_and(k > 0, k < nk - 1))
        def _():
            o_ref[...] += part

        @pl.when(k == nk - 1)
        def _():
            o_ref[...] = finish(o_ref[...] + part)


def _matmul(x, w, *, tm, tn=512, tk=None, layer=None, n_cols=None, col_off=0, epilogue="none", resid=None,
            out_dtype=F32):
    m, kdim = x.shape
    n = w.shape[-1] if n_cols is None else n_cols
    tn = min(tn, n)
    tk = kdim if tk is None else min(tk, kdim)
    assert m % tm == 0 and n % tn == 0 and kdim % tk == 0 and col_off % tn == 0
    nk = kdim // tk
    assert nk == 1 or out_dtype == F32
    joff = col_off // tn
    x_mode = pl.Buffered(1) if (nk == 1 and n // tn > 1 and tm * tk * 2 > SINGLE_BUFFER_BYTES) else None
    if layer is None:
        w_spec = pl.BlockSpec((tk, tn), lambda i, j, k: (k, j + joff))
    else:
        w_spec = pl.BlockSpec((None, tk, tn), lambda i, j, k: (layer, k, j + joff))
    in_specs = [pl.BlockSpec((tm, tk), lambda i, j, k: (i, k), pipeline_mode=x_mode), w_spec]
    args = [x, w]
    if epilogue == "resid":
        r_mode = pl.Buffered(1) if (nk > 1 and tm * tn * 4 > SINGLE_BUFFER_BYTES) else None
        in_specs.append(pl.BlockSpec((tm, tn), lambda i, j, k: (i, j), pipeline_mode=r_mode))
        args.append(resid)
    return pl.pallas_call(
        functools.partial(_mm_kernel, nk=nk, epilogue=epilogue),
        grid=(m // tm, n // tn, nk),
        in_specs=in_specs,
        out_specs=pl.BlockSpec((tm, tn), lambda i, j, k: (i, j)),
        out_shape=jax.ShapeDtypeStruct((m, n), out_dtype),
        compiler_params=_params(("parallel", "parallel", "arbitrary")),
    )(*args)


def _shift_cols_kernel(a_ref, b_ref, o_ref, *, shift):
    w = jnp.concatenate([a_ref[...], b_ref[...]], axis=1)
    width = w.shape[1]
    o_ref[...] = pltpu.roll(w, width - shift, axis=1)[:, :o_ref.shape[1]].astype(o_ref.dtype)


def _shifted_cols_bf16(w, layer, col0, n_cols, *, tr=512, tn=1024):
    kdim = w.shape[1]
    shift = col0 % LANES
    base = col0 - shift
    assert shift > 0 and base % tn == 0 and n_cols % tn == 0 and kdim % tr == 0
    assert base + n_cols + shift <= w.shape[2]
    return pl.pallas_call(
        functools.partial(_shift_cols_kernel, shift=shift),
        grid=(kdim // tr, n_cols // tn),
        in_specs=[pl.BlockSpec((None, tr, tn), lambda i, j: (layer, i, base // tn + j)),
                  pl.BlockSpec((None, tr, LANES), lambda i, j: (layer, i, (base + (j + 1) * tn) // LANES))],
        out_specs=pl.BlockSpec((tr, tn), lambda i, j: (i, j)),
        out_shape=jax.ShapeDtypeStruct((kdim, n_cols), BF16),
        compiler_params=_params(("parallel", "parallel")),
    )(w, w)


def _merge_kernel(a_ref, b_ref, c_ref, wa_ref, wb_ref, wc_ref, ga_ref, gb_ref, gc_ref, o_ref):
    acc = _sigmoid(ga_ref[...]) * _dot(a_ref[...], wa_ref[...].astype(BF16))
    acc += _sigmoid(gb_ref[...]) * _dot(b_ref[...], wb_ref[...].astype(BF16))
    acc += _sigmoid(gc_ref[...]) * _dot(c_ref[...], wc_ref[...].astype(BF16))
    o_ref[...] = acc.astype(o_ref.dtype)


def _merge(a, b, c, wa, wb, wc, gates, *, layer, tm, tn=256):
    m, kdim = a.shape
    n = wa.shape[-1]
    assert m % tm == 0 and n % tn == 0 and gates.shape == (m, N_BRANCH * n)
    nj = n // tn
    x_mode = pl.Buffered(1) if tm * kdim * 2 > SINGLE_BUFFER_BYTES else None

    def x_spec():
        return pl.BlockSpec((tm, kdim), lambda i, j: (i, 0), pipeline_mode=x_mode)

    def g_spec(branch):
        return pl.BlockSpec((tm, tn), lambda i, j: (i, j + branch * nj))

    w_spec = pl.BlockSpec((None, kdim, tn), lambda i, j: (layer, 0, j))
    return pl.pallas_call(
        _merge_kernel,
        grid=(m // tm, nj),
        in_specs=[x_spec(), x_spec(), x_spec(), w_spec, w_spec, w_spec, g_spec(0), g_spec(1), g_spec(2)],
        out_specs=pl.BlockSpec((tm, tn), lambda i, j: (i, j)),
        out_shape=jax.ShapeDtypeStruct((m, n), BF16),
        compiler_params=_params(("parallel", "parallel")),
    )(a, b, c, wa, wb, wc, gates, gates, gates)


def _gmlp_kernel(u_ref, v_ref, w_ref, b_ref, g_ref, a_ref, *va_refs, chunk, chunks, gdim):
    row = lax.broadcasted_iota(jnp.int32, (chunk, chunk), 0)
    col = lax.broadcasted_iota(jnp.int32, (chunk, chunk), 1)
    causal = col <= row
    w_groups = [jnp.where(causal, w_ref[g], 0.0).astype(BF16) for g in range(A_GROUPS)]
    gain = g_ref[...]
    bias = b_ref[...]
    for c in range(chunks):
        rows = pl.ds(c * chunk, chunk)
        v = _gelu(v_ref[rows, :])
        vn = v * lax.rsqrt(jnp.mean(v * v, axis=-1, keepdims=True) + EPS) * gain
        if va_refs:
            va_refs[0][rows, :] = vn
        vb = vn.astype(BF16)
        for g in range(A_GROUPS):
            cols = slice(g * gdim, (g + 1) * gdim)
            mixed = _dot(w_groups[g], vb[:, cols]) + bias[:, g:g + 1]
            u = _gelu(u_ref[rows, cols])
            a_ref[rows, cols] = (u * mixed).astype(a_ref.dtype)


def _gmlp(proj, ws, bs_t, norm_g, *, width, chunk, chunks, row0, n_rows, out_rows, want_va):
    rows = chunk * chunks
    assert n_rows % rows == 0 and row0 % rows == 0
    blk0 = row0 // rows
    gdim = width // A_GROUPS
    out_shape = [jax.ShapeDtypeStruct((out_rows, width), BF16)]
    out_specs = [pl.BlockSpec((rows, width), lambda j: (j, 0))]
    if want_va:
        out_shape.append(jax.ShapeDtypeStruct((out_rows, width), F32))
        out_specs.append(pl.BlockSpec((rows, width), lambda j: (j, 0)))
    res = pl.pallas_call(
        functools.partial(_gmlp_kernel, chunk=chunk, chunks=chunks, gdim=gdim),
        grid=(n_rows // rows,),
        in_specs=[pl.BlockSpec((rows, width), lambda j: (blk0 + j, 0)),
                  pl.BlockSpec((rows, width), lambda j: (blk0 + j, 1)),
                  pl.BlockSpec((A_GROUPS, chunk, chunk), lambda j: (0, 0, 0)),
                  pl.BlockSpec((chunk, A_GROUPS), lambda j: (0, 0)),
                  pl.BlockSpec((1, width), lambda j: (0, 0))],
        out_specs=out_specs,
        out_shape=out_shape,
        compiler_params=_params(("parallel",)),
    )(proj, proj, ws, bs_t, norm_g.reshape(1, width))
    return res if want_va else (res[0], None)


def _qk_norm(x, g, scale):
    return x * lax.rsqrt(jnp.mean(x * x, axis=-1, keepdims=True) + EPS) * (g * scale)


ATTN_GROUP = 4


def _attn_prompt_kernel(*refs, seq, aliased):
    q_ref, k_ref, v_ref, qg_ref, kg_ref = refs[:5]
    o_ref, ko_ref, vo_ref, qp, kp, vp, os_, ls = refs[5 + (2 if aliased else 0):]
    v = v_ref[...]
    kn = _qk_norm(k_ref[...], kg_ref[...], 1.0)
    qn = _qk_norm(q_ref[...], qg_ref[...], B_HEAD_DIM ** -0.5)
    ko_ref[...] = kn
    vo_ref[...] = v
    data = pl.ds(B_BLOCK, seq)
    for p in range(len(B_PATTERNS)):
        kp[p, 0:B_BLOCK, :] = jnp.zeros((B_BLOCK, B_HEAD_DIM), BF16)
        vp[p, 0:B_BLOCK, :] = jnp.zeros((B_BLOCK, 2 * B_HEAD_DIM), BF16)
        vp[p, data, B_HEAD_DIM:] = jnp.ones((seq, B_HEAD_DIM), BF16)
    qp[0] = qn.astype(BF16)
    kp[0, data, :] = kn.astype(BF16)
    vp[0, data, 0:B_HEAD_DIM] = v.astype(BF16)
    os_[0] = qn
    for p, (win, dil) in enumerate(B_PATTERNS):
        if dil == 1:
            continue
        ln = seq // dil
        for r in range(dil):
            src = pl.ds(r, ln, stride=dil)
            qp[p, pl.ds(r * ln, ln), :] = os_[0, src, :].astype(BF16)
            kp[p, pl.ds(B_BLOCK + r * ln, ln), :] = ko_ref[src, :].astype(BF16)
            vp[p, pl.ds(B_BLOCK + r * ln, ln), 0:B_HEAD_DIM] = v_ref[src, :].astype(BF16)

    qi = lax.broadcasted_iota(jnp.int32, (B_BLOCK, B_BLOCK), 0)
    kj = lax.broadcasted_iota(jnp.int32, (B_BLOCK, B_BLOCK), 1)
    mask_cur = kj <= qi
    mask_prev = kj >= qi

    def group(p, dil, g0, firsts, position0):
        n_q = len(firsts) * B_BLOCK
        base = g0 * B_BLOCK
        if not isinstance(base, int):
            base = pl.multiple_of(base, B_BLOCK)
        s = _dot_nt(qp[p, pl.ds(base, n_q), :], kp[p, pl.ds(base, n_q + B_BLOCK), :])
        for u, first in enumerate(firsts):
            lo, mid, hi = u * B_BLOCK, (u + 1) * B_BLOCK, (u + 2) * B_BLOCK
            s_cur = jnp.where(mask_cur, s[lo:mid, mid:hi], NEG_INF)
            if first:
                m = jnp.max(s_cur, axis=-1, keepdims=True)
                res = _dot(jnp.exp(s_cur - m).astype(BF16), vp[p, pl.ds(base + mid, B_BLOCK), :])
            else:
                s_prev = jnp.where(mask_prev, s[lo:mid, lo:mid], NEG_INF)
                m = jnp.max(jnp.maximum(s_cur, s_prev), axis=-1, keepdims=True)
                probs = jnp.concatenate([jnp.exp(s_prev - m).astype(BF16), jnp.exp(s_cur - m).astype(BF16)], axis=1)
                res = _dot(probs, vp[p, pl.ds(base + lo, 2 * B_BLOCK), :])
            den = res[:, B_HEAD_DIM:]
            rows = pl.ds(position0(u), B_BLOCK, stride=dil) if dil > 1 else pl.ds(base + lo, B_BLOCK)
            os_[p, rows, :] = res[:, :B_HEAD_DIM] * (1.0 / den)
            ls[p, rows, :] = m + jnp.log(den)

    for p, (win, dil) in enumerate(B_PATTERNS):
        nb = seq // (dil * B_BLOCK)
        if nb == 1:
            def body(t, carry, p=p, dil=dil):
                group(p, dil, t * ATTN_GROUP, (True,) * ATTN_GROUP, lambda u: t * ATTN_GROUP + u)
                return carry

            lax.fori_loop(0, dil // ATTN_GROUP, body, 0)
            continue
        head = (True,) + (False,) * (ATTN_GROUP - 1)
        tail = (False,) * ATTN_GROUP
        per_class = nb // ATTN_GROUP

        def one_class(r, carry, p=p, dil=dil, nb=nb, per_class=per_class, head=head, tail=tail):
            group(p, dil, r * nb, head, lambda u: u * (B_BLOCK * dil) + r)
            if per_class > 1:
                def rest(t, c2):
                    group(p, dil, r * nb + t * ATTN_GROUP, tail,
                          lambda u: (t * ATTN_GROUP + u) * (B_BLOCK * dil) + r)
                    return c2

                lax.fori_loop(1, per_class, rest, 0)
            return carry

        if dil == 1:
            one_class(0, 0)
        else:
            lax.fori_loop(0, dil, one_class, 0)

    l0, l1, l2 = ls[0], ls[1], ls[2]
    mx = jnp.maximum(jnp.maximum(l0, l1), l2)
    e0, e1, e2 = jnp.exp(l0 - mx), jnp.exp(l1 - mx), jnp.exp(l2 - mx)
    out = (e0 * os_[0] + e1 * os_[1] + e2 * os_[2]) * (1.0 / (e0 + e1 + e2))
    o_ref[...] = out.astype(o_ref.dtype)


def _attn_prompt(proj, qg, kg, *, n_seq, seq, col0, out_rows, layer, depth, kv_prev=None):
    assert B_PATTERNS[0][1] == 1 and len(B_PATTERNS) == 3
    for _, dil in B_PATTERNS:
        nb = seq // (dil * B_BLOCK)
        assert seq % (dil * B_BLOCK) == 0 and (nb % ATTN_GROUP == 0 or (nb == 1 and dil % ATTN_GROUP == 0))
    width = B_HEADS * B_HEAD_DIM
    blk = (seq, B_HEAD_DIM)

    def in_spec(part):
        return pl.BlockSpec(blk, lambda i, h: (i, col0 + part * B_HEADS + h))

    g_spec = pl.BlockSpec((1, B_HEAD_DIM), lambda i, h: (0, 0))
    kv_spec = pl.BlockSpec((None, None, seq, B_HEAD_DIM), lambda i, h: (layer, i, 0, h))
    kv_shape = jax.ShapeDtypeStruct((depth, n_seq, seq, width), F32)
    in_specs = [in_spec(0), in_spec(1), in_spec(2), g_spec, g_spec]
    args = [proj, proj, proj, qg.reshape(1, -1), kg.reshape(1, -1)]
    aliases = {}
    if kv_prev is not None:
        in_specs += [pl.BlockSpec(memory_space=pl.ANY)] * 2
        args += list(kv_prev)
        aliases = {5: 1, 6: 2}
    n_pat = len(B_PATTERNS)
    return pl.pallas_call(
        functools.partial(_attn_prompt_kernel, seq=seq, aliased=kv_prev is not None),
        grid=(n_seq, B_HEADS),
        in_specs=in_specs,
        out_specs=[pl.BlockSpec(blk, lambda i, h: (i, h)), kv_spec, kv_spec],
        out_shape=[jax.ShapeDtypeStruct((out_rows, width), BF16), kv_shape, kv_shape],
        input_output_aliases=aliases,
        scratch_shapes=[pltpu.VMEM((n_pat, seq, B_HEAD_DIM), BF16),
                        pltpu.VMEM((n_pat, seq + B_BLOCK, B_HEAD_DIM), BF16),
                        pltpu.VMEM((n_pat, seq + B_BLOCK, 2 * B_HEAD_DIM), BF16)]
        + [pltpu.VMEM((n_pat, seq, B_HEAD_DIM), F32)] * 2,
        compiler_params=_params(("parallel", "parallel")),
    )(*args)


SAMPLE_ROWS = 8
SAMPLE_HEADS_PER_STEP = 4


def _attn_sample_kernel(q_ref, k_ref, v_ref, kc_ref, vc_ref, qg_ref, kg_ref, o_ref, ko_ref, vo_ref,
                        q8, k8, v8, *, t_new, buf):
    t_c = lax.broadcasted_iota(jnp.int32, (SAMPLE_ROWS, buf), 0)
    r_c = lax.broadcasted_iota(jnp.int32, (SAMPLE_ROWS, buf), 1)
    dist_c = buf + t_c - r_c
    t_n = lax.broadcasted_iota(jnp.int32, (SAMPLE_ROWS, SAMPLE_ROWS), 0)
    u_n = lax.broadcasted_iota(jnp.int32, (SAMPLE_ROWS, SAMPLE_ROWS), 1)
    dist_n = t_n - u_n
    q8[...] = jnp.zeros_like(q8)
    k8[...] = jnp.zeros_like(k8)
    v8[...] = jnp.zeros_like(v8)
    for hh in range(SAMPLE_HEADS_PER_STEP):
        cols = slice(hh * B_HEAD_DIM, (hh + 1) * B_HEAD_DIM)
        kn = _qk_norm(k_ref[0, :, cols], kg_ref[...], 1.0)
        ko_ref[0, :, cols] = kn
        vo_ref[0, :, cols] = v_ref[0, :, cols]
        q8[0:t_new, :] = _qk_norm(q_ref[0, :, cols], qg_ref[...], B_HEAD_DIM ** -0.5)
        k8[0:t_new, :] = kn
        v8[0:t_new, :] = v_ref[0, :, cols]

        qb = q8[...].astype(BF16)
        s_c = _dot_nt(qb, kc_ref[0, :, cols].astype(BF16))
        s_n = _dot_nt(qb, k8[...].astype(BF16))
        vcb = vc_ref[0, :, cols].astype(BF16)
        vnew = v8[...].astype(BF16).astype(F32)

        outs, lses = [], []
        for win, dil in B_PATTERNS:
            ok_c = jnp.logical_and((dist_c & (dil - 1)) == 0, dist_c <= win)
            ok_n = jnp.logical_and(jnp.logical_and(dist_n >= 0, (dist_n & (dil - 1)) == 0), u_n < t_new)
            sc = jnp.where(ok_c, s_c, NEG_INF)
            sn = jnp.where(ok_n, s_n, NEG_INF)
            m = jnp.maximum(jnp.max(sc, axis=-1, keepdims=True), jnp.max(sn, axis=-1, keepdims=True))
            pc = jnp.exp(sc - m)
            pn = jnp.exp(sn - m)
            den = jnp.sum(pc, axis=-1, keepdims=True) + jnp.sum(pn, axis=-1, keepdims=True)
            num = _dot(pc.astype(BF16), vcb)
            pn = pn.astype(BF16).astype(F32)
            for u in range(t_new):
                num = num + pn[:, u:u + 1] * vnew[u:u + 1, :]
            outs.append(num * (1.0 / den))
            lses.append(m + jnp.log(den))
        mx = jnp.maximum(jnp.maximum(lses[0], lses[1]), lses[2])
        es = [jnp.exp(l - mx) for l in lses]
        out = (es[0] * outs[0] + es[1] * outs[1] + es[2] * outs[2]) * (1.0 / (es[0] + es[1] + es[2]))
        o_ref[0, :, cols] = out[0:t_new, :]


def _attn_sample(qkv, cache_k, cache_v, qg, kg, *, layer):
    n, t_new, _ = qkv.shape
    buf = cache_k.shape[2]
    assert t_new <= SAMPLE_ROWS and all(win <= buf for win, _ in B_PATTERNS)
    assert all(dil & (dil - 1) == 0 for _, dil in B_PATTERNS)
    width = B_HEADS * B_HEAD_DIM
    hw = SAMPLE_HEADS_PER_STEP * B_HEAD_DIM
    steps = width // hw
    blk = (1, t_new, hw)

    def in_spec(part):
        return pl.BlockSpec(blk, lambda i, h: (i, 0, part * steps + h))

    c_spec = pl.BlockSpec((None, 1, buf, hw), lambda i, h: (layer, i, 0, h))
    g_spec = pl.BlockSpec((1, B_HEAD_DIM), lambda i, h: (0, 0))
    out_spec = pl.BlockSpec(blk, lambda i, h: (i, 0, h))
    return pl.pallas_call(
        functools.partial(_attn_sample_kernel, t_new=t_new, buf=buf),
        grid=(n, steps),
        in_specs=[in_spec(0), in_spec(1), in_spec(2), c_spec, c_spec, g_spec, g_spec],
        out_specs=[out_spec, out_spec, out_spec],
        out_shape=[jax.ShapeDtypeStruct((n, t_new, width), F32)] * 3,
        scratch_shapes=[pltpu.VMEM((SAMPLE_ROWS, B_HEAD_DIM), F32)] * 3,
        compiler_params=_params(("parallel", "parallel")),
    )(qkv, qkv, qkv, cache_k, cache_v, qg.reshape(1, -1), kg.reshape(1, -1))


def _split3(x):
    hi = x.astype(BF16)
    r1 = x - hi.astype(F32)
    mid = r1.astype(BF16)
    lo = (r1 - mid.astype(F32)).astype(BF16)
    return hi, mid, lo


def _mlstm_kernel(q_ref, k_ref, v_ref, o_ref, ifc_ref, ifr_ref, bc_ref, br_ref, hg_ref, c0_ref, n0_ref, m0_ref,
                  h_ref, cf_ref, nf_ref, mf_ref, ct_s, n_s, m_s, *, ln):
    j = pl.program_id(1)
    last = pl.num_programs(1) - 1

    @pl.when(j == 0)
    def _():
        for h in range(C_HEADS):
            ct_s[h] = c0_ref[0, h].T
        n_s[...] = n0_ref[0]
        m_s[...] = m0_ref[0]

    ifc = ifc_ref[...] + bc_ref[...]
    ifr = ifr_ref[0] + br_ref[...]
    row = lax.broadcasted_iota(jnp.int32, (ln, ln), 0)
    col = lax.broadcasted_iota(jnp.int32, (ln, ln), 1)
    causal = col <= row
    tri = jnp.where(causal, 1.0, 0.0).astype(BF16)
    triu = jnp.where(row <= col, 1.0, 0.0).astype(BF16)
    b_col = sum(_dot(tri, part) for part in _split3(_log_sigmoid(ifc)))
    b_row = sum(_dot(part, triu) for part in _split3(_log_sigmoid(ifr)))

    for h in range(C_HEADS):
        qc = q_ref[:, h * C_QK_DIM:(h + 1) * C_QK_DIM]
        qb = qc.astype(BF16)
        kc = k_ref[:, h * C_QK_DIM:(h + 1) * C_QK_DIM] * (C_QK_DIM ** -0.5)
        kb = kc.astype(BF16)
        vc = v_ref[:, h * C_V_DIM:(h + 1) * C_V_DIM]
        bcol = b_col[:, C_HEADS + h:C_HEADS + h + 1]
        brow = b_row[C_HEADS + h:C_HEADS + h + 1, :]
        icol = ifc[:, h:h + 1]
        irow = ifr[h:h + 1, :]
        m_prev = m_s[h:h + 1, 0:1]
        n_prev = n_s[h:h + 1, :]
        ct_prev = ct_s[h]

        dmat = jnp.where(causal, bcol - brow + irow, NEG_INF)
        inter = bcol + m_prev
        mt = jnp.maximum(inter, jnp.max(dmat, axis=-1, keepdims=True))
        wmat = jnp.exp(dmat - mt) * _dot_nt(qb, kb)
        w_inter = jnp.exp(inter - mt)
        num = _dot(wmat.astype(BF16), vc.astype(BF16)) + w_inter * _dot(qb, ct_prev.astype(BF16))
        den = jnp.sum(wmat, axis=-1, keepdims=True) + w_inter * jnp.sum(qc * n_prev, axis=-1, keepdims=True)
        hc = num * (1.0 / jnp.maximum(jnp.abs(den), jnp.exp(-mt)))

        m_new = mt[ln - 1:ln, :]
        b_last = bcol[ln - 1:ln, :]
        w_end = jnp.exp(b_last - bcol + icol - m_new)
        decay = jnp.exp(b_last + m_prev - m_new)
        ct_s[h] = decay * ct_prev + _dot(kc.T.astype(BF16), (w_end * vc).astype(BF16))
        n_s[h:h + 1, :] = decay * n_prev + jnp.sum(w_end * kc, axis=0, keepdims=True)
        m_s[h:h + 1, :] = jnp.broadcast_to(m_new, (1, LANES))

        gain = hg_ref[:, h * C_V_DIM:(h + 1) * C_V_DIM]
        hn = hc * lax.rsqrt(jnp.mean(hc * hc, axis=-1, keepdims=True) + EPS) * gain
        og = _sigmoid(o_ref[:, h * C_V_DIM:(h + 1) * C_V_DIM])
        h_ref[:, h * C_V_DIM:(h + 1) * C_V_DIM] = (og * hn).astype(h_ref.dtype)

    @pl.when(j == last)
    def _():
        for h in range(C_HEADS):
            cf_ref[0, h] = ct_s[h].T
        nf_ref[0] = n_s[...]
        mf_ref[0] = m_s[...]


def _mlstm(proj, ifc, ifr, bias_c, bias_r, hn_g, c0, n0, m0, *, n_seq, seq, col_q, col_k, col_v, col_o, ln,
           out_rows):
    ln = min(ln, seq)
    nc = seq // ln
    assert seq % ln == 0 and ln % LANES == 0
    qk_w = C_HEADS * C_QK_DIM
    v_w = C_HEADS * C_V_DIM
    assert col_q % qk_w == 0 and col_k % qk_w == 0 and col_v % v_w == 0 and col_o % v_w == 0
    state_specs = [pl.BlockSpec((1, C_HEADS, C_V_DIM, C_QK_DIM), lambda i, j: (i, 0, 0, 0)),
                   pl.BlockSpec((1, C_HEADS, C_QK_DIM), lambda i, j: (i, 0, 0)),
                   pl.BlockSpec((1, C_HEADS, LANES), lambda i, j: (i, 0, 0))]
    return pl.pallas_call(
        functools.partial(_mlstm_kernel, ln=ln),
        grid=(n_seq, nc),
        in_specs=[pl.BlockSpec((ln, qk_w), lambda i, j: (i * nc + j, col_q // qk_w)),
                  pl.BlockSpec((ln, qk_w), lambda i, j: (i * nc + j, col_k // qk_w)),
                  pl.BlockSpec((ln, v_w), lambda i, j: (i * nc + j, col_v // v_w)),
                  pl.BlockSpec((ln, v_w), lambda i, j: (i * nc + j, col_o // v_w)),
                  pl.BlockSpec((ln, LANES), lambda i, j: (i * nc + j, 0)),
                  pl.BlockSpec((1, 2 * C_HEADS, ln), lambda i, j: (i, 0, j)),
                  pl.BlockSpec((1, LANES), lambda i, j: (0, 0)),
                  pl.BlockSpec((2 * C_HEADS, 1), lambda i, j: (0, 0)),
                  pl.BlockSpec((1, v_w), lambda i, j: (0, 0))] + state_specs,
        out_specs=[pl.BlockSpec((ln, v_w), lambda i, j: (i * nc + j, 0))] + state_specs,
        out_shape=[jax.ShapeDtypeStruct((out_rows, v_w), BF16),
                   jax.ShapeDtypeStruct((n_seq, C_HEADS, C_V_DIM, C_QK_DIM), F32),
                   jax.ShapeDtypeStruct((n_seq, C_HEADS, C_QK_DIM), F32),
                   jax.ShapeDtypeStruct((n_seq, C_HEADS, LANES), F32)],
        scratch_shapes=[pltpu.VMEM((C_HEADS, C_QK_DIM, C_V_DIM), F32),
                        pltpu.VMEM((C_HEADS, C_QK_DIM), F32),
                        pltpu.VMEM((C_HEADS, LANES), F32)],
        compiler_params=_params(("parallel", "arbitrary")),
    )(proj, proj, proj, proj, ifc, ifr, bias_c, bias_r, hn_g.reshape(1, v_w), c0, n0, m0)


MLSTM_CHUNK = 256
GATE_PAD = 1e30
ROW_TILES = 4
MERGE_ROW_TILES = 6


def _put_rows(full, rows, row0, n_pad):
    block = jnp.pad(rows.astype(full.dtype), ((0, n_pad - rows.shape[0]), (0, 0)))
    return lax.dynamic_update_slice(full, block, (row0, 0))


def kernel(x_prompt, x_sample, cache_swa_k, cache_swa_v, state_mlstm_C, state_mlstm_n, state_mlstm_m, norm1_g, w_in, ws_a, bs_a, norm_va_g, qn_g, kn_g, i_b, f_b, hn_c_g, w_branch_a, w_branch_b, w_branch_c, w_out, norm2_g, w_ff1, w_ff2):
    depth = w_in.shape[0]
    nbp, seq, d = x_prompt.shape
    nbs, t_new, _ = x_sample.shape
    buf = cache_swa_k.shape[2]
    hd = B_HEADS * B_HEAD_DIM
    a_width = d // 2
    qk_w = C_HEADS * C_QK_DIM
    v_w = C_HEADS * C_V_DIM
    n_gate = N_BRANCH * d
    n_main = w_in.shape[2] - n_gate - IF_COLS
    col_b = 2 * a_width
    col_c = col_b + 3 * hd
    assert col_c + 2 * qk_w + 2 * v_w == n_main and seq % A_CHUNK == 0 and t_new <= A_CHUNK

    mp, ms = nbp * seq, nbs * t_new
    quantum = BF16_SUBLANES * math.lcm(ROW_TILES, MERGE_ROW_TILES)
    m_tot = -(-(mp + ms) // quantum) * quantum
    n_tail = m_tot - mp
    tm = m_tot // ROW_TILES
    tm_merge = m_tot // MERGE_ROW_TILES
    tm_norm = m_tot // (ROW_TILES * 3)
    assert mp % n_tail == 0 and n_tail % t_new == 0 and tm_norm % BF16_SUBLANES == 0
    x = jnp.concatenate([x_prompt.reshape(mp, d), x_sample.reshape(ms, d), jnp.zeros((m_tot - mp - ms, d), F32)])

    cache_k = cache_swa_k.reshape(depth, nbs, buf, hd)
    cache_v = cache_swa_v.reshape(depth, nbs, buf, hd)
    zero_state = (jnp.zeros((nbp, C_HEADS, C_V_DIM, C_QK_DIM), F32), jnp.zeros((nbp, C_HEADS, C_QK_DIM), F32),
                  jnp.zeros((nbp, C_HEADS, LANES), F32))
    lane = jnp.arange(LANES)
    gate_pad_row = jnp.where(lane < C_HEADS, -GATE_PAD, jnp.where(lane < IF_COLS, GATE_PAD, 0.0)).astype(F32)
    seqs_tail = n_tail // t_new

    kv_prompt = None
    small = [[] for _ in range(9)]
    for l in range(depth):
        w_if = jnp.pad(w_in[l, :, n_main:n_main + IF_COLS], ((0, 0), (0, LANES - IF_COLS)))
        w_gates = _shifted_cols_bf16(w_in, l, n_main + IF_COLS, n_gate)
        bias = jnp.concatenate([i_b[l], f_b[l]])
        bias_c = jnp.pad(bias, (0, LANES - IF_COLS)).reshape(1, LANES)
        bias_r = bias.reshape(IF_COLS, 1)

        xn = _rmsnorm_bf16(x, norm1_g[l], tm=tm_norm)
        proj = _matmul(xn, w_in, layer=l, n_cols=n_main, tm=tm)
        ifp = _matmul(xn, w_if, tm=tm)
        gates = _matmul(xn, w_gates, tm=tm)

        a_out, _ = _gmlp(proj, ws_a[l], bs_a[l].T, norm_va_g[l], width=a_width, chunk=A_CHUNK, chunks=2,
                         row0=0, n_rows=mp, out_rows=m_tot, want_va=False)
        eye = jnp.eye(seqs_tail, dtype=F32)
        ws_tail = jax.vmap(lambda w: jnp.kron(eye, w[:t_new, :t_new]))(ws_a[l])
        bs_tail = jnp.tile(bs_a[l][:, :t_new], (1, seqs_tail)).T
        a_tail, va_tail = _gmlp(proj, ws_tail, bs_tail, norm_va_g[l], width=a_width, chunk=n_tail, chunks=1,
                                row0=mp, n_rows=n_tail, out_rows=n_tail, want_va=True)
        a_out = lax.dynamic_update_slice(a_out, a_tail, (mp, 0))

        b_out, kp, vp = _attn_prompt(proj, qn_g[l], kn_g[l], n_seq=nbp, seq=seq, col0=col_b // B_HEAD_DIM,
                                     out_rows=m_tot, layer=l, depth=depth, kv_prev=kv_prompt)
        kv_prompt = (kp, vp)
        qkv_s = proj[mp:mp + ms, col_b:col_c].reshape(nbs, t_new, 3 * hd)
        b_s, k_s, v_s = _attn_sample(qkv_s, cache_k, cache_v, qn_g[l], kn_g[l], layer=l)
        b_out = _put_rows(b_out, b_s.reshape(ms, hd), mp, n_tail)

        cols = dict(col_q=col_c, col_k=col_c + qk_w, col_v=col_c + 2 * qk_w, col_o=col_c + 2 * qk_w + v_w)
        ifr = jnp.swapaxes(ifp[:mp, :IF_COLS].reshape(nbp, seq, IF_COLS), 1, 2)
        c_out, cp, n_p, m_p = _mlstm(proj, ifp, ifr, bias_c, bias_r, hn_c_g[l], *zero_state, n_seq=nbp, seq=seq,
                                     ln=MLSTM_CHUNK, out_rows=m_tot, **cols)
        pad = A_CHUNK - t_new
        proj_s = jnp.pad(proj[mp:mp + ms, col_c:].reshape(nbs, t_new, n_main - col_c), ((0, 0), (0, pad), (0, 0)))
        ifc_s = jnp.concatenate([ifp[mp:mp + ms].reshape(nbs, t_new, LANES),
                                 jnp.broadcast_to(gate_pad_row, (nbs, pad, LANES))], axis=1)
        ifr_s = jnp.swapaxes(ifc_s[:, :, :IF_COLS], 1, 2)
        m0 = jnp.broadcast_to(state_mlstm_m[l][..., None], (nbs, C_HEADS, LANES))
        c_s_out, c_s, n_s, m_s = _mlstm(
            proj_s.reshape(nbs * A_CHUNK, -1), ifc_s.reshape(nbs * A_CHUNK, LANES), ifr_s, bias_c, bias_r, hn_c_g[l],
            state_mlstm_C[l], state_mlstm_n[l], m0, n_seq=nbs, seq=A_CHUNK, ln=A_CHUNK, out_rows=nbs * A_CHUNK,
            col_q=0, col_k=qk_w, col_v=2 * qk_w, col_o=2 * qk_w + v_w)
        c_out = _put_rows(c_out, c_s_out.reshape(nbs, A_CHUNK, v_w)[:, :t_new].reshape(ms, v_w), mp, n_tail)

        merged = _merge(a_out, b_out, c_out, w_branch_a, w_branch_b, w_branch_c, gates, layer=l, tm=tm_merge)
        x = _matmul(merged, w_out, layer=l, epilogue="resid", resid=x, tm=tm, tn=256)
        h = _matmul(_rmsnorm_bf16(x, norm2_g[l], tm=tm_norm), w_ff1, layer=l, epilogue="relu2", out_dtype=BF16, tm=tm)
        x = _matmul(h, w_ff2, layer=l, epilogue="resid", resid=x, tm=tm, tn=1024, tk=1024)

        vals = (cp, n_p, m_p[..., 0], k_s.reshape(nbs, t_new, B_HEADS, B_HEAD_DIM),
                v_s.reshape(nbs, t_new, B_HEADS, B_HEAD_DIM), c_s, n_s, m_s[..., 0],
                va_tail[:ms].reshape(nbs, t_new, a_width))
        for o, v in zip(small, vals):
            o.append(v)

    keep = min(max(w for w, _ in B_PATTERNS), seq)
    kp, vp = (t[:, :, seq - keep:].reshape(depth, nbp, keep, B_HEADS, B_HEAD_DIM) for t in kv_prompt)
    st = [jnp.stack(o) for o in small]
    return (x[:mp].reshape(nbp, seq, d), x[mp:mp + ms].reshape(nbs, t_new, d), kp, vp, st[0], st[1], st[2],
            st[3], st[4], st[5], st[6], st[7], st[8])
```

```python
import functools
import math

import jax
import jax.numpy as jnp
from jax import lax
from jax.experimental import pallas as pl
from jax.experimental.pallas import tpu as pltpu

F32 = jnp.float32
BF16 = jnp.bfloat16

EPS = 1e-6
NEG_INF = -1e30

A_CHUNK = 128
A_GROUPS = 8
B_HEADS = 16
B_HEAD_DIM = 128
B_PATTERNS = ((128, 1), (512, 4), (2048, 16))
B_BLOCK = 128
C_HEADS = 8
C_QK_DIM = 128
C_V_DIM = 256
N_BRANCH = 3
IF_COLS = 2 * C_HEADS

V7X_VMEM_BYTES = 64 * 1024 * 1024
VMEM_LIMIT_BYTES = V7X_VMEM_BYTES - 8 * 1024 * 1024
LANES = 128
BF16_SUBLANES = 16
SINGLE_BUFFER_BYTES = 4 << 20


def _params(semantics):
    return pltpu.CompilerParams(dimension_semantics=semantics, vmem_limit_bytes=VMEM_LIMIT_BYTES)


def _gelu(x):
    c = math.sqrt(2.0 / math.pi)
    return 0.5 * x * (1.0 + jnp.tanh(c * (x + 0.044715 * (x * x * x))))


def _sigmoid(x):
    return 1.0 / (1.0 + jnp.exp(-x))


def _log_sigmoid(x):
    return -(jnp.maximum(-x, 0.0) + jnp.log(1.0 + jnp.exp(-jnp.abs(x))))


def _dot(a, b):
    return jnp.dot(a, b, preferred_element_type=F32)


def _dot_nt(a, b):
    return lax.dot_general(a, b, (((1,), (1,)), ((), ())), preferred_element_type=F32)


def _rmsnorm_kernel(x_ref, g_ref, o_ref):
    x = x_ref[...]
    y = x * lax.rsqrt(jnp.mean(x * x, axis=-1, keepdims=True) + EPS)
    o_ref[...] = (y * g_ref[...]).astype(o_ref.dtype)


def _rmsnorm_bf16(x, g, *, tm):
    m, d = x.shape
    assert m % tm == 0
    return pl.pallas_call(
        _rmsnorm_kernel,
        grid=(m // tm,),
        in_specs=[pl.BlockSpec((tm, d), lambda i: (i, 0)), pl.BlockSpec((1, d), lambda i: (0, 0))],
        out_specs=pl.BlockSpec((tm, d), lambda i: (i, 0)),
        out_shape=jax.ShapeDtypeStruct((m, d), BF16),
        compiler_params=_params(("parallel",)),
    )(x, g.reshape(1, d))


def _mm_kernel(x_ref, w_ref, *rest, nk, epilogue, w_transposed):
    o_ref = rest[-1]
    extra = rest[:-1]
    w = w_ref[...].astype(BF16)
    part = _dot_nt(x_ref[...], w) if w_transposed else _dot(x_ref[...], w)
    if nk == 1:
        if epilogue == "relu2":
            r = jnp.maximum(part, 0.0)
            part = r * r
        elif epilogue == "resid":
            part = part + extra[0][...]
        o_ref[...] = part.astype(o_ref.dtype)
    else:
        @pl.when(pl.program_id(2) == 0)
        def _():
            o_ref[...] = extra[0][...] if epilogue == "resid" else jnp.zeros_like(o_ref)

        o_ref[...] += part


def _matmul(x, w, *, tm, tn=512, tk=None, layer=None, n_cols=None, col_off=0, epilogue="none", resid=None,
            out_dtype=F32, w_transposed=False):
    m, kdim = x.shape
    n = w.shape[-2 if w_transposed else -1] if n_cols is None else n_cols
    tn = min(tn, n)
    tk = kdim if tk is None else min(tk, kdim)
    assert m % tm == 0 and n % tn == 0 and kdim % tk == 0 and col_off % tn == 0
    nk = kdim // tk
    assert nk == 1 or (out_dtype == F32 and epilogue in ("none", "resid"))
    joff = col_off // tn
    x_mode = pl.Buffered(1) if (nk == 1 and n // tn > 1 and tm * tk * 2 > SINGLE_BUFFER_BYTES) else None
    w_blk = (tn, tk) if w_transposed else (tk, tn)
    if w_transposed:
        w_idx = lambda i, j, k: (j + joff, k)
    else:
        w_idx = lambda i, j, k: (k, j + joff)
    if layer is None:
        w_spec = pl.BlockSpec(w_blk, w_idx)
    else:
        w_spec = pl.BlockSpec((None,) + w_blk, lambda i, j, k: (layer,) + w_idx(i, j, k))
    in_specs = [pl.BlockSpec((tm, tk), lambda i, j, k: (i, k), pipeline_mode=x_mode), w_spec]
    args = [x, w]
    if epilogue == "resid":
        r_mode = pl.Buffered(1) if (nk > 1 and tm * tn * 4 > SINGLE_BUFFER_BYTES) else None
        in_specs.append(pl.BlockSpec((tm, tn), lambda i, j, k: (i, j), pipeline_mode=r_mode))
        args.append(resid)
    return pl.pallas_call(
        functools.partial(_mm_kernel, nk=nk, epilogue=epilogue, w_transposed=w_transposed),
        grid=(m // tm, n // tn, nk),
        in_specs=in_specs,
        out_specs=pl.BlockSpec((tm, tn), lambda i, j, k: (i, j)),
        out_shape=jax.ShapeDtypeStruct((m, n), out_dtype),
        compiler_params=_params(("parallel", "parallel", "arbitrary")),
    )(*args)


def _shift_rows_kernel(a_ref, b_ref, o_ref, *, shift):
    o_ref[...] = jnp.concatenate([a_ref[shift:, :], b_ref[...]], axis=0).astype(o_ref.dtype)


def _shifted_rows_bf16(w, layer, row0, n_rows, *, tr=512, tc=1024):
    kdim = w.shape[2]
    shift = row0 % tr
    base = row0 - shift
    assert shift > 0 and shift % 8 == 0 and tr % shift == 0 and base % shift == 0
    assert n_rows % tr == 0 and kdim % tc == 0 and row0 + n_rows <= w.shape[1]
    return pl.pallas_call(
        functools.partial(_shift_rows_kernel, shift=shift),
        grid=(n_rows // tr, kdim // tc),
        in_specs=[pl.BlockSpec((None, tr, tc), lambda i, j: (layer, base // tr + i, j)),
                  pl.BlockSpec((None, shift, tc), lambda i, j: (layer, (base + (i + 1) * tr) // shift, j))],
        out_specs=pl.BlockSpec((tr, tc), lambda i, j: (i, j)),
        out_shape=jax.ShapeDtypeStruct((n_rows, kdim), BF16),
        compiler_params=_params(("parallel", "parallel")),
    )(w, w)


def _merge_kernel(a_ref, b_ref, c_ref, wa_ref, wb_ref, wc_ref, ga_ref, gb_ref, gc_ref, o_ref):
    acc = _sigmoid(ga_ref[...]) * _dot(a_ref[...], wa_ref[...].astype(BF16))
    acc += _sigmoid(gb_ref[...]) * _dot(b_ref[...], wb_ref[...].astype(BF16))
    acc += _sigmoid(gc_ref[...]) * _dot(c_ref[...], wc_ref[...].astype(BF16))
    o_ref[...] = acc.astype(o_ref.dtype)


def _merge(a, b, c, wa, wb, wc, gates, *, layer, tm, tn=256):
    m, kdim = a.shape
    n = wa.shape[-1]
    assert m % tm == 0 and n % tn == 0 and gates.shape == (m, N_BRANCH * n)
    nj = n // tn
    x_mode = pl.Buffered(1) if tm * kdim * 2 > SINGLE_BUFFER_BYTES else None

    def x_spec():
        return pl.BlockSpec((tm, kdim), lambda i, j: (i, 0), pipeline_mode=x_mode)

    def g_spec(branch):
        return pl.BlockSpec((tm, tn), lambda i, j: (i, j + branch * nj))

    w_spec = pl.BlockSpec((None, kdim, tn), lambda i, j: (layer, 0, j))
    return pl.pallas_call(
        _merge_kernel,
        grid=(m // tm, nj),
        in_specs=[x_spec(), x_spec(), x_spec(), w_spec, w_spec, w_spec, g_spec(0), g_spec(1), g_spec(2)],
        out_specs=pl.BlockSpec((tm, tn), lambda i, j: (i, j)),
        out_shape=jax.ShapeDtypeStruct((m, n), BF16),
        compiler_params=_params(("parallel", "parallel")),
    )(a, b, c, wa, wb, wc, gates, gates, gates)


def _gmlp_kernel(u_ref, v_ref, w_ref, b_ref, g_ref, a_ref, *va_refs, chunk, chunks, gdim):
    row = lax.broadcasted_iota(jnp.int32, (chunk, chunk), 0)
    col = lax.broadcasted_iota(jnp.int32, (chunk, chunk), 1)
    causal = col <= row
    w_groups = [jnp.where(causal, w_ref[g], 0.0).astype(BF16) for g in range(A_GROUPS)]
    gain = g_ref[...]
    bias = b_ref[...]
    for c in range(chunks):
        rows = pl.ds(c * chunk, chunk)
        v = _gelu(v_ref[rows, :])
        vn = v * lax.rsqrt(jnp.mean(v * v, axis=-1, keepdims=True) + EPS) * gain
        if va_refs:
            va_refs[0][rows, :] = vn
        vb = vn.astype(BF16)
        for g in range(A_GROUPS):
            cols = slice(g * gdim, (g + 1) * gdim)
            mixed = _dot(w_groups[g], vb[:, cols]) + bias[:, g:g + 1]
            u = _gelu(u_ref[rows, cols])
            a_ref[rows, cols] = (u * mixed).astype(a_ref.dtype)


def _gmlp(proj, ws, bs_t, norm_g, *, width, chunk, chunks, row0, n_rows, out_rows, want_va):
    rows = chunk * chunks
    assert n_rows % rows == 0 and row0 % rows == 0
    blk0 = row0 // rows
    gdim = width // A_GROUPS
    out_shape = [jax.ShapeDtypeStruct((out_rows, width), BF16)]
    out_specs = [pl.BlockSpec((rows, width), lambda j: (j, 0))]
    if want_va:
        out_shape.append(jax.ShapeDtypeStruct((out_rows, width), F32))
        out_specs.append(pl.BlockSpec((rows, width), lambda j: (j, 0)))
    res = pl.pallas_call(
        functools.partial(_gmlp_kernel, chunk=chunk, chunks=chunks, gdim=gdim),
        grid=(n_rows // rows,),
        in_specs=[pl.BlockSpec((rows, width), lambda j: (blk0 + j, 0)),
                  pl.BlockSpec((rows, width), lambda j: (blk0 + j, 1)),
                  pl.BlockSpec((A_GROUPS, chunk, chunk), lambda j: (0, 0, 0)),
                  pl.BlockSpec((chunk, A_GROUPS), lambda j: (0, 0)),
                  pl.BlockSpec((1, width), lambda j: (0, 0))],
        out_specs=out_specs,
        out_shape=out_shape,
        compiler_params=_params(("parallel",)),
    )(proj, proj, ws, bs_t, norm_g.reshape(1, width))
    return res if want_va else (res[0], None)


def _qk_norm(x, g, scale):
    return x * lax.rsqrt(jnp.mean(x * x, axis=-1, keepdims=True) + EPS) * (g * scale)


ATTN_GROUP = 4


def _attn_prompt_kernel(*refs, seq, aliased):
    q_ref, k_ref, v_ref, qg_ref, kg_ref = refs[:5]
    o_ref, ko_ref, vo_ref, qp, kp, vp, os_, ls = refs[5 + (2 if aliased else 0):]
    v = v_ref[...]
    kn = _qk_norm(k_ref[...], kg_ref[...], 1.0)
    qn = _qk_norm(q_ref[...], qg_ref[...], B_HEAD_DIM ** -0.5)
    ko_ref[...] = kn
    vo_ref[...] = v
    data = pl.ds(B_BLOCK, seq)
    for p in range(len(B_PATTERNS)):
        kp[p, 0:B_BLOCK, :] = jnp.zeros((B_BLOCK, B_HEAD_DIM), BF16)
        vp[p, 0:B_BLOCK, :] = jnp.zeros((B_BLOCK, 2 * B_HEAD_DIM), BF16)
        vp[p, data, B_HEAD_DIM:] = jnp.ones((seq, B_HEAD_DIM), BF16)
    qp[0] = qn.astype(BF16)
    kp[0, data, :] = kn.astype(BF16)
    vp[0, data, 0:B_HEAD_DIM] = v.astype(BF16)
    os_[0] = qn
    for p, (win, dil) in enumerate(B_PATTERNS):
        if dil == 1:
            continue
        ln = seq // dil
        for r in range(dil):
            src = pl.ds(r, ln, stride=dil)
            qp[p, pl.ds(r * ln, ln), :] = os_[0, src, :].astype(BF16)
            kp[p, pl.ds(B_BLOCK + r * ln, ln), :] = ko_ref[src, :].astype(BF16)
            vp[p, pl.ds(B_BLOCK + r * ln, ln), 0:B_HEAD_DIM] = v_ref[src, :].astype(BF16)

    qi = lax.broadcasted_iota(jnp.int32, (B_BLOCK, B_BLOCK), 0)
    kj = lax.broadcasted_iota(jnp.int32, (B_BLOCK, B_BLOCK), 1)
    mask_cur = kj <= qi
    mask_prev = kj >= qi

    def group(p, dil, g0, firsts, position0):
        n_q = len(firsts) * B_BLOCK
        base = g0 * B_BLOCK
        if not isinstance(base, int):
            base = pl.multiple_of(base, B_BLOCK)
        s = _dot_nt(qp[p, pl.ds(base, n_q), :], kp[p, pl.ds(base, n_q + B_BLOCK), :])
        for u, first in enumerate(firsts):
            lo, mid, hi = u * B_BLOCK, (u + 1) * B_BLOCK, (u + 2) * B_BLOCK
            s_cur = jnp.where(mask_cur, s[lo:mid, mid:hi], NEG_INF)
            if first:
                m = jnp.max(s_cur, axis=-1, keepdims=True)
                res = _dot(jnp.exp(s_cur - m).astype(BF16), vp[p, pl.ds(base + mid, B_BLOCK), :])
            else:
                s_prev = jnp.where(mask_prev, s[lo:mid, lo:mid], NEG_INF)
                m = jnp.max(jnp.maximum(s_cur, s_prev), axis=-1, keepdims=True)
                probs = jnp.concatenate([jnp.exp(s_prev - m).astype(BF16), jnp.exp(s_cur - m).astype(BF16)], axis=1)
                res = _dot(probs, vp[p, pl.ds(base + lo, 2 * B_BLOCK), :])
            den = res[:, B_HEAD_DIM:]
            rows = pl.ds(position0(u), B_BLOCK, stride=dil) if dil > 1 else pl.ds(base + lo, B_BLOCK)
            os_[p, rows, :] = res[:, :B_HEAD_DIM] * (1.0 / den)
            ls[p, rows, :] = m + jnp.log(den)

    for p, (win, dil) in enumerate(B_PATTERNS):
        nb = seq // (dil * B_BLOCK)
        if nb == 1:
            def body(t, carry, p=p, dil=dil):
                group(p, dil, t * ATTN_GROUP, (True,) * ATTN_GROUP, lambda u: t * ATTN_GROUP + u)
                return carry

            lax.fori_loop(0, dil // ATTN_GROUP, body, 0)
            continue
        head = (True,) + (False,) * (ATTN_GROUP - 1)
        tail = (False,) * ATTN_GROUP
        per_class = nb // ATTN_GROUP

        def one_class(r, carry, p=p, dil=dil, nb=nb, per_class=per_class, head=head, tail=tail):
            group(p, dil, r * nb, head, lambda u: u * (B_BLOCK * dil) + r)
            if per_class > 1:
                def rest(t, c2):
                    group(p, dil, r * nb + t * ATTN_GROUP, tail,
                          lambda u: (t * ATTN_GROUP + u) * (B_BLOCK * dil) + r)
                    return c2

                lax.fori_loop(1, per_class, rest, 0)
            return carry

        if dil == 1:
            one_class(0, 0)
        else:
            lax.fori_loop(0, dil, one_class, 0)

    l0, l1, l2 = ls[0], ls[1], ls[2]
    mx = jnp.maximum(jnp.maximum(l0, l1), l2)
    e0, e1, e2 = jnp.exp(l0 - mx), jnp.exp(l1 - mx), jnp.exp(l2 - mx)
    out = (e0 * os_[0] + e1 * os_[1] + e2 * os_[2]) * (1.0 / (e0 + e1 + e2))
    o_ref[...] = out.astype(o_ref.dtype)


def _attn_prompt(proj, qg, kg, *, n_seq, seq, col0, out_rows, layer, depth, kv_prev=None):
    assert B_PATTERNS[0][1] == 1 and len(B_PATTERNS) == 3
    for _, dil in B_PATTERNS:
        nb = seq // (dil * B_BLOCK)
        assert seq % (dil * B_BLOCK) == 0 and (nb % ATTN_GROUP == 0 or (nb == 1 and dil % ATTN_GROUP == 0))
    width = B_HEADS * B_HEAD_DIM
    blk = (seq, B_HEAD_DIM)

    def in_spec(part):
        return pl.BlockSpec(blk, lambda i, h: (i, col0 + part * B_HEADS + h))

    g_spec = pl.BlockSpec((1, B_HEAD_DIM), lambda i, h: (0, 0))
    kv_spec = pl.BlockSpec((None, None, seq, B_HEAD_DIM), lambda i, h: (layer, i, 0, h))
    kv_shape = jax.ShapeDtypeStruct((depth, n_seq, seq, width), F32)
    in_specs = [in_spec(0), in_spec(1), in_spec(2), g_spec, g_spec]
    args = [proj, proj, proj, qg.reshape(1, -1), kg.reshape(1, -1)]
    aliases = {}
    if kv_prev is not None:
        in_specs += [pl.BlockSpec(memory_space=pl.ANY)] * 2
        args += list(kv_prev)
        aliases = {5: 1, 6: 2}
    n_pat = len(B_PATTERNS)
    return pl.pallas_call(
        functools.partial(_attn_prompt_kernel, seq=seq, aliased=kv_prev is not None),
        grid=(n_seq, B_HEADS),
        in_specs=in_specs,
        out_specs=[pl.BlockSpec(blk, lambda i, h: (i, h)), kv_spec, kv_spec],
        out_shape=[jax.ShapeDtypeStruct((out_rows, width), BF16), kv_shape, kv_shape],
        input_output_aliases=aliases,
        scratch_shapes=[pltpu.VMEM((n_pat, seq, B_HEAD_DIM), BF16),
                        pltpu.VMEM((n_pat, seq + B_BLOCK, B_HEAD_DIM), BF16),
                        pltpu.VMEM((n_pat, seq + B_BLOCK, 2 * B_HEAD_DIM), BF16)]
        + [pltpu.VMEM((n_pat, seq, B_HEAD_DIM), F32)] * 2,
        compiler_params=_params(("parallel", "parallel")),
    )(*args)


SAMPLE_HEADS_PER_STEP = 8


def _divmod_const(x, d):
    if d & (d - 1) == 0:
        return x >> (d.bit_length() - 1), x & (d - 1)
    return x // d, x % d


def _attn_sample_kernel(q_ref, k_ref, v_ref, kc_ref, vc_ref, qg_ref, kg_ref, o_ref, ko_ref, vo_ref,
                        qs, ks, vs, *, t_new, buf):
    hs = SAMPLE_HEADS_PER_STEP
    n_rows = hs * t_new
    for hh in range(hs):
        cols = slice(hh * B_HEAD_DIM, (hh + 1) * B_HEAD_DIM)
        rows = slice(hh * t_new, (hh + 1) * t_new)
        kn = _qk_norm(k_ref[0, :, cols], kg_ref[...], 1.0)
        ko_ref[0, :, cols] = kn
        vo_ref[0, :, cols] = v_ref[0, :, cols]
        qs[rows, :] = _qk_norm(q_ref[0, :, cols], qg_ref[...], B_HEAD_DIM ** -0.5)
        ks[rows, :] = kn
        vs[rows, :] = v_ref[0, :, cols]

    qb = qs[...].astype(BF16)
    s_c = _dot_nt(qb, kc_ref[...].reshape(buf * hs, B_HEAD_DIM).astype(BF16))
    s_n = _dot_nt(qb, ks[...].astype(BF16))
    vcb = vc_ref[...].reshape(buf * hs, B_HEAD_DIM).astype(BF16)
    vnew = vs[...].astype(BF16)

    row_c = lax.broadcasted_iota(jnp.int32, (n_rows, buf * hs), 0)
    col_c = lax.broadcasted_iota(jnp.int32, (n_rows, buf * hs), 1)
    head_r, tok_r = _divmod_const(row_c, t_new)
    cache_row, head_c = _divmod_const(col_c, hs)
    same_c = head_r == head_c
    dist_c = buf + tok_r - cache_row
    row_n = lax.broadcasted_iota(jnp.int32, (n_rows, n_rows), 0)
    col_n = lax.broadcasted_iota(jnp.int32, (n_rows, n_rows), 1)
    head_rn, tok_rn = _divmod_const(row_n, t_new)
    head_cn, tok_cn = _divmod_const(col_n, t_new)
    dist_n = tok_rn - tok_cn
    same_n = jnp.logical_and(head_rn == head_cn, dist_n >= 0)

    outs, lses = [], []
    for win, dil in B_PATTERNS:
        ok_c = jnp.logical_and(same_c, jnp.logical_and((dist_c & (dil - 1)) == 0, dist_c <= win))
        ok_n = jnp.logical_and(same_n, (dist_n & (dil - 1)) == 0)
        sc = jnp.where(ok_c, s_c, NEG_INF)
        sn = jnp.where(ok_n, s_n, NEG_INF)
        m = jnp.maximum(jnp.max(sc, axis=-1, keepdims=True), jnp.max(sn, axis=-1, keepdims=True))
        pc = jnp.exp(sc - m)
        pn = jnp.exp(sn - m)
        den = jnp.sum(pc, axis=-1, keepdims=True) + jnp.sum(pn, axis=-1, keepdims=True)
        num = _dot(pc.astype(BF16), vcb) + _dot(pn.astype(BF16), vnew)
        outs.append(num * (1.0 / den))
        lses.append(m + jnp.log(den))
    mx = jnp.maximum(jnp.maximum(lses[0], lses[1]), lses[2])
    es = [jnp.exp(l - mx) for l in lses]
    out = (es[0] * outs[0] + es[1] * outs[1] + es[2] * outs[2]) * (1.0 / (es[0] + es[1] + es[2]))
    for hh in range(hs):
        o_ref[0, :, hh * B_HEAD_DIM:(hh + 1) * B_HEAD_DIM] = out[hh * t_new:(hh + 1) * t_new, :]


def _attn_sample(qkv, cache_k, cache_v, qg, kg, *, layer):
    n, t_new, _ = qkv.shape
    buf = cache_k.shape[2]
    assert cache_k.shape[3:] == (B_HEADS, B_HEAD_DIM)
    assert all(win <= buf for win, _ in B_PATTERNS) and all(dil & (dil - 1) == 0 for _, dil in B_PATTERNS)
    assert (SAMPLE_HEADS_PER_STEP * t_new) % 8 == 0
    width = B_HEADS * B_HEAD_DIM
    hw = SAMPLE_HEADS_PER_STEP * B_HEAD_DIM
    steps = width // hw
    blk = (1, t_new, hw)

    def in_spec(part):
        return pl.BlockSpec(blk, lambda i, h: (i, 0, part * steps + h))

    c_spec = pl.BlockSpec((None, None, buf, SAMPLE_HEADS_PER_STEP, B_HEAD_DIM), lambda i, h: (layer, i, 0, h, 0))
    g_spec = pl.BlockSpec((1, B_HEAD_DIM), lambda i, h: (0, 0))
    out_spec = pl.BlockSpec(blk, lambda i, h: (i, 0, h))
    return pl.pallas_call(
        functools.partial(_attn_sample_kernel, t_new=t_new, buf=buf),
        grid=(n, steps),
        in_specs=[in_spec(0), in_spec(1), in_spec(2), c_spec, c_spec, g_spec, g_spec],
        out_specs=[out_spec, out_spec, out_spec],
        out_shape=[jax.ShapeDtypeStruct((n, t_new, width), F32)] * 3,
        scratch_shapes=[pltpu.VMEM((SAMPLE_HEADS_PER_STEP * t_new, B_HEAD_DIM), F32)] * 3,
        compiler_params=_params(("parallel", "parallel")),
    )(qkv, qkv, qkv, cache_k, cache_v, qg.reshape(1, -1), kg.reshape(1, -1))


def _split3(x):
    hi = x.astype(BF16)
    r1 = x - hi.astype(F32)
    mid = r1.astype(BF16)
    lo = (r1 - mid.astype(F32)).astype(BF16)
    return hi, mid, lo


def _mlstm_kernel(q_ref, k_ref, v_ref, o_ref, ifc_ref, ifr_ref, bc_ref, br_ref, hg_ref, c0_ref, n0_ref, m0_ref,
                  h_ref, cf_ref, nf_ref, mf_ref, ct_s, n_s, m_s, *, ln):
    j = pl.program_id(1)
    last = pl.num_programs(1) - 1

    @pl.when(j == 0)
    def _():
        for h in range(C_HEADS):
            ct_s[h] = c0_ref[0, h].T
        n_s[...] = n0_ref[0]
        m_s[...] = m0_ref[0]

    ifc = ifc_ref[...] + bc_ref[...]
    ifr = ifr_ref[0] + br_ref[...]
    row = lax.broadcasted_iota(jnp.int32, (ln, ln), 0)
    col = lax.broadcasted_iota(jnp.int32, (ln, ln), 1)
    causal = col <= row
    tri = jnp.where(causal, 1.0, 0.0).astype(BF16)
    triu = jnp.where(row <= col, 1.0, 0.0).astype(BF16)
    b_col = sum(_dot(tri, part) for part in _split3(_log_sigmoid(ifc)))
    b_row = sum(_dot(part, triu) for part in _split3(_log_sigmoid(ifr)))

    for h in range(C_HEADS):
        qc = q_ref[:, h * C_QK_DIM:(h + 1) * C_QK_DIM]
        qb = qc.astype(BF16)
        kc = k_ref[:, h * C_QK_DIM:(h + 1) * C_QK_DIM] * (C_QK_DIM ** -0.5)
        kb = kc.astype(BF16)
        vc = v_ref[:, h * C_V_DIM:(h + 1) * C_V_DIM]
        bcol = b_col[:, C_HEADS + h:C_HEADS + h + 1]
        brow = b_row[C_HEADS + h:C_HEADS + h + 1, :]
        icol = ifc[:, h:h + 1]
        irow = ifr[h:h + 1, :]
        m_prev = m_s[h:h + 1, 0:1]
        n_prev = n_s[h:h + 1, :]
        ct_prev = ct_s[h]

        dmat = jnp.where(causal, bcol - brow + irow, NEG_INF)
        inter = bcol + m_prev
        mt = jnp.maximum(inter, jnp.max(dmat, axis=-1, keepdims=True))
        wmat = jnp.exp(dmat - mt) * _dot_nt(qb, kb)
        w_inter = jnp.exp(inter - mt)
        num = _dot(wmat.astype(BF16), vc.astype(BF16)) + w_inter * _dot(qb, ct_prev.astype(BF16))
        den = jnp.sum(wmat, axis=-1, keepdims=True) + w_inter * jnp.sum(qc * n_prev, axis=-1, keepdims=True)
        hc = num * (1.0 / jnp.maximum(jnp.abs(den), jnp.exp(-mt)))

        m_new = mt[ln - 1:ln, :]
        b_last = bcol[ln - 1:ln, :]
        w_end = jnp.exp(b_last - bcol + icol - m_new)
        decay = jnp.exp(b_last + m_prev - m_new)
        ct_s[h] = decay * ct_prev + _dot(kc.T.astype(BF16), (w_end * vc).astype(BF16))
        n_s[h:h + 1, :] = decay * n_prev + jnp.sum(w_end * kc, axis=0, keepdims=True)
        m_s[h:h + 1, :] = jnp.broadcast_to(m_new, (1, LANES))

        gain = hg_ref[:, h * C_V_DIM:(h + 1) * C_V_DIM]
        hn = hc * lax.rsqrt(jnp.mean(hc * hc, axis=-1, keepdims=True) + EPS) * gain
        og = _sigmoid(o_ref[:, h * C_V_DIM:(h + 1) * C_V_DIM])
        h_ref[:, h * C_V_DIM:(h + 1) * C_V_DIM] = (og * hn).astype(h_ref.dtype)

    @pl.when(j == last)
    def _():
        for h in range(C_HEADS):
            cf_ref[0, h] = ct_s[h].T
        nf_ref[0] = n_s[...]
        mf_ref[0] = m_s[...]


def _mlstm(proj, ifc, ifr, bias_c, bias_r, hn_g, c0, n0, m0, *, n_seq, seq, col_q, col_k, col_v, col_o, ln,
           out_rows):
    ln = min(ln, seq)
    nc = seq // ln
    assert seq % ln == 0 and ln % LANES == 0
    qk_w = C_HEADS * C_QK_DIM
    v_w = C_HEADS * C_V_DIM
    assert col_q % qk_w == 0 and col_k % qk_w == 0 and col_v % v_w == 0 and col_o % v_w == 0
    state_specs = [pl.BlockSpec((1, C_HEADS, C_V_DIM, C_QK_DIM), lambda i, j: (i, 0, 0, 0)),
                   pl.BlockSpec((1, C_HEADS, C_QK_DIM), lambda i, j: (i, 0, 0)),
                   pl.BlockSpec((1, C_HEADS, LANES), lambda i, j: (i, 0, 0))]
    return pl.pallas_call(
        functools.partial(_mlstm_kernel, ln=ln),
        grid=(n_seq, nc),
        in_specs=[pl.BlockSpec((ln, qk_w), lambda i, j: (i * nc + j, col_q // qk_w)),
                  pl.BlockSpec((ln, qk_w), lambda i, j: (i * nc + j, col_k // qk_w)),
                  pl.BlockSpec((ln, v_w), lambda i, j: (i * nc + j, col_v // v_w)),
                  pl.BlockSpec((ln, v_w), lambda i, j: (i * nc + j, col_o // v_w)),
                  pl.BlockSpec((ln, LANES), lambda i, j: (i * nc + j, 0)),
                  pl.BlockSpec((1, 2 * C_HEADS, ln), lambda i, j: (i, 0, j)),
                  pl.BlockSpec((1, LANES), lambda i, j: (0, 0)),
                  pl.BlockSpec((2 * C_HEADS, 1), lambda i, j: (0, 0)),
                  pl.BlockSpec((1, v_w), lambda i, j: (0, 0))] + state_specs,
        out_specs=[pl.BlockSpec((ln, v_w), lambda i, j: (i * nc + j, 0))] + state_specs,
        out_shape=[jax.ShapeDtypeStruct((out_rows, v_w), BF16),
                   jax.ShapeDtypeStruct((n_seq, C_HEADS, C_V_DIM, C_QK_DIM), F32),
                   jax.ShapeDtypeStruct((n_seq, C_HEADS, C_QK_DIM), F32),
                   jax.ShapeDtypeStruct((n_seq, C_HEADS, LANES), F32)],
        scratch_shapes=[pltpu.VMEM((C_HEADS, C_QK_DIM, C_V_DIM), F32),
                        pltpu.VMEM((C_HEADS, C_QK_DIM), F32),
                        pltpu.VMEM((C_HEADS, LANES), F32)],
        compiler_params=_params(("parallel", "arbitrary")),
    )(proj, proj, proj, proj, ifc, ifr, bias_c, bias_r, hn_g.reshape(1, v_w), c0, n0, m0)


MLSTM_CHUNK = 256
GATE_PAD = 1e30
ROW_TILES = 4
MERGE_ROW_TILES = 6


def _put_rows(full, rows, row0, n_pad):
    block = jnp.pad(rows.astype(full.dtype), ((0, n_pad - rows.shape[0]), (0, 0)))
    return lax.dynamic_update_slice(full, block, (row0, 0))


def kernel(x_prompt, x_sample, cache_swa_k, cache_swa_v, state_mlstm_C, state_mlstm_n, state_mlstm_m, norm1_g, w_in, ws_a, bs_a, norm_va_g, qn_g, kn_g, i_b, f_b, hn_c_g, w_branch_a, w_branch_b, w_branch_c, w_out, norm2_g, w_ff1, w_ff2):
    depth = w_in.shape[0]
    nbp, seq, d = x_prompt.shape
    nbs, t_new, _ = x_sample.shape
    buf = cache_swa_k.shape[2]
    hd = B_HEADS * B_HEAD_DIM
    a_width = d // 2
    qk_w = C_HEADS * C_QK_DIM
    v_w = C_HEADS * C_V_DIM
    n_gate = N_BRANCH * d
    n_main = w_in.shape[2] - n_gate - IF_COLS
    col_b = 2 * a_width
    col_c = col_b + 3 * hd
    assert col_c + 2 * qk_w + 2 * v_w == n_main and seq % A_CHUNK == 0 and t_new <= A_CHUNK

    mp, ms = nbp * seq, nbs * t_new
    quantum = BF16_SUBLANES * math.lcm(ROW_TILES, MERGE_ROW_TILES)
    m_tot = -(-(mp + ms) // quantum) * quantum
    n_tail = m_tot - mp
    tm = m_tot // ROW_TILES
    tm_merge = m_tot // MERGE_ROW_TILES
    tm_norm = m_tot // (ROW_TILES * 3)
    assert mp % n_tail == 0 and n_tail % t_new == 0 and tm_norm % BF16_SUBLANES == 0
    x = jnp.concatenate([x_prompt.reshape(mp, d), x_sample.reshape(ms, d), jnp.zeros((m_tot - mp - ms, d), F32)])

    w_in_t = jnp.swapaxes(w_in, 1, 2)
    zero_state = (jnp.zeros((nbp, C_HEADS, C_V_DIM, C_QK_DIM), F32), jnp.zeros((nbp, C_HEADS, C_QK_DIM), F32),
                  jnp.zeros((nbp, C_HEADS, LANES), F32))
    lane = jnp.arange(LANES)
    gate_pad_row = jnp.where(lane < C_HEADS, -GATE_PAD, jnp.where(lane < IF_COLS, GATE_PAD, 0.0)).astype(F32)
    seqs_tail = n_tail // t_new

    kv_prompt = None
    small = [[] for _ in range(9)]
    for l in range(depth):
        w_if = jnp.pad(w_in_t[l, n_main:n_main + IF_COLS, :], ((0, LANES - IF_COLS), (0, 0)))
        w_gates = _shifted_rows_bf16(w_in_t, l, n_main + IF_COLS, n_gate)
        bias = jnp.concatenate([i_b[l], f_b[l]])
        bias_c = jnp.pad(bias, (0, LANES - IF_COLS)).reshape(1, LANES)
        bias_r = bias.reshape(IF_COLS, 1)

        xn = _rmsnorm_bf16(x, norm1_g[l], tm=tm_norm)
        proj = _matmul(xn, w_in_t, layer=l, n_cols=n_main, tm=tm, w_transposed=True)
        ifp = _matmul(xn, w_if, tm=tm, w_transposed=True)
        gates = _matmul(xn, w_gates, tm=tm, w_transposed=True)

        a_out, _ = _gmlp(proj, ws_a[l], bs_a[l].T, norm_va_g[l], width=a_width, chunk=A_CHUNK, chunks=2,
                         row0=0, n_rows=mp, out_rows=m_tot, want_va=False)
        eye = jnp.eye(seqs_tail, dtype=F32)
        ws_tail = jax.vmap(lambda w: jnp.kron(eye, w[:t_new, :t_new]))(ws_a[l])
        bs_tail = jnp.tile(bs_a[l][:, :t_new], (1, seqs_tail)).T
        a_tail, va_tail = _gmlp(proj, ws_tail, bs_tail, norm_va_g[l], width=a_width, chunk=n_tail, chunks=1,
                                row0=mp, n_rows=n_tail, out_rows=n_tail, want_va=True)
        a_out = lax.dynamic_update_slice(a_out, a_tail, (mp, 0))

        b_out, kp, vp = _attn_prompt(proj, qn_g[l], kn_g[l], n_seq=nbp, seq=seq, col0=col_b // B_HEAD_DIM,
                                     out_rows=m_tot, layer=l, depth=depth, kv_prev=kv_prompt)
        kv_prompt = (kp, vp)
        qkv_s = proj[mp:mp + ms, col_b:col_c].reshape(nbs, t_new, 3 * hd)
        b_s, k_s, v_s = _attn_sample(qkv_s, cache_swa_k, cache_swa_v, qn_g[l], kn_g[l], layer=l)
        b_out = _put_rows(b_out, b_s.reshape(ms, hd), mp, n_tail)

        cols = dict(col_q=col_c, col_k=col_c + qk_w, col_v=col_c + 2 * qk_w, col_o=col_c + 2 * qk_w + v_w)
        ifr = jnp.swapaxes(ifp[:mp, :IF_COLS].reshape(nbp, seq, IF_COLS), 1, 2)
        c_out, cp, n_p, m_p = _mlstm(proj, ifp, ifr, bias_c, bias_r, hn_c_g[l], *zero_state, n_seq=nbp, seq=seq,
                                     ln=MLSTM_CHUNK, out_rows=m_tot, **cols)
        pad = A_CHUNK - t_new
        proj_s = jnp.pad(proj[mp:mp + ms, col_c:].reshape(nbs, t_new, n_main - col_c), ((0, 0), (0, pad), (0, 0)))
        ifc_s = jnp.concatenate([ifp[mp:mp + ms].reshape(nbs, t_new, LANES),
                                 jnp.broadcast_to(gate_pad_row, (nbs, pad, LANES))], axis=1)
        ifr_s = jnp.swapaxes(ifc_s[:, :, :IF_COLS], 1, 2)
        m0 = jnp.broadcast_to(state_mlstm_m[l][..., None], (nbs, C_HEADS, LANES))
        c_s_out, c_s, n_s, m_s = _mlstm(
            proj_s.reshape(nbs * A_CHUNK, -1), ifc_s.reshape(nbs * A_CHUNK, LANES), ifr_s, bias_c, bias_r, hn_c_g[l],
            state_mlstm_C[l], state_mlstm_n[l], m0, n_seq=nbs, seq=A_CHUNK, ln=A_CHUNK, out_rows=nbs * A_CHUNK,
            col_q=0, col_k=qk_w, col_v=2 * qk_w, col_o=2 * qk_w + v_w)
        c_out = _put_rows(c_out, c_s_out.reshape(nbs, A_CHUNK, v_w)[:, :t_new].reshape(ms, v_w), mp, n_tail)

        merged = _merge(a_out, b_out, c_out, w_branch_a, w_branch_b, w_branch_c, gates, layer=l, tm=tm_merge)
        x = _matmul(merged, w_out, layer=l, epilogue="resid", resid=x, tm=tm, tn=256)
        h = _matmul(_rmsnorm_bf16(x, norm2_g[l], tm=tm_norm), w_ff1, layer=l, epilogue="relu2", out_dtype=BF16, tm=tm)
        x = _matmul(h, w_ff2, layer=l, epilogue="resid", resid=x, tm=tm, tn=1024, tk=1024)

        vals = (cp, n_p, m_p[..., 0], k_s.reshape(nbs, t_new, B_HEADS, B_HEAD_DIM),
                v_s.reshape(nbs, t_new, B_HEADS, B_HEAD_DIM), c_s, n_s, m_s[..., 0],
                va_tail[:ms].reshape(nbs, t_new, a_width))
        for o, v in zip(small, vals):
            o.append(v)

    keep = min(max(w for w, _ in B_PATTERNS), seq)
    kp, vp = (t[:, :, seq - keep:].reshape(depth, nbp, keep, B_HEADS, B_HEAD_DIM) for t in kv_prompt)
    st = [jnp.stack(o) for o in small]
    return (x[:mp].reshape(nbp, seq, d), x[mp:mp + ms].reshape(nbs, t_new, d), kp, vp, st[0], st[1], st[2],
            st[3], st[4], st[5], st[6], st[7], st[8])
```

```python
import functools
import math

import jax
import jax.numpy as jnp
from jax import lax
from jax.experimental import pallas as pl
from jax.experimental.pallas import tpu as pltpu

F32 = jnp.float32
BF16 = jnp.bfloat16

EPS = 1e-6
NEG_INF = -1e30

A_CHUNK = 128
A_GROUPS = 8
B_HEADS = 16
B_HEAD_DIM = 128
B_PATTERNS = ((128, 1), (512, 4), (2048, 16))
B_BLOCK = 128
C_HEADS = 8
C_QK_DIM = 128
C_V_DIM = 256
N_BRANCH = 3
IF_COLS = 2 * C_HEADS

V7X_VMEM_BYTES = 64 * 1024 * 1024
VMEM_LIMIT_BYTES = V7X_VMEM_BYTES - 8 * 1024 * 1024
LANES = 128
BF16_SUBLANES = 16
SINGLE_BUFFER_BYTES = 4 << 20


def _params(semantics):
    return pltpu.CompilerParams(dimension_semantics=semantics, vmem_limit_bytes=VMEM_LIMIT_BYTES)


def _gelu(x):
    c = math.sqrt(2.0 / math.pi)
    return 0.5 * x * (1.0 + jnp.tanh(c * (x + 0.044715 * (x * x * x))))


def _sigmoid(x):
    return 1.0 / (1.0 + jnp.exp(-x))


def _log_sigmoid(x):
    return -(jnp.maximum(-x, 0.0) + jnp.log(1.0 + jnp.exp(-jnp.abs(x))))


def _dot(a, b):
    return jnp.dot(a, b, preferred_element_type=F32)


def _dot_nt(a, b):
    return lax.dot_general(a, b, (((1,), (1,)), ((), ())), preferred_element_type=F32)


def _rmsnorm_kernel(x_ref, g_ref, o_ref):
    x = x_ref[...]
    y = x * lax.rsqrt(jnp.mean(x * x, axis=-1, keepdims=True) + EPS)
    o_ref[...] = (y * g_ref[...]).astype(o_ref.dtype)


def _rmsnorm_bf16(x, g, *, tm):
    m, d = x.shape
    assert m % tm == 0
    return pl.pallas_call(
        _rmsnorm_kernel,
        grid=(m // tm,),
        in_specs=[pl.BlockSpec((tm, d), lambda i: (i, 0)), pl.BlockSpec((1, d), lambda i: (0, 0))],
        out_specs=pl.BlockSpec((tm, d), lambda i: (i, 0)),
        out_shape=jax.ShapeDtypeStruct((m, d), BF16),
        compiler_params=_params(("parallel",)),
    )(x, g.reshape(1, d))


def _mm_kernel(x_ref, w_ref, *rest, epilogue, w_transposed, row_shift):
    o_ref = rest[-1]
    extra = rest[:-1]
    w = w_ref[...]
    if row_shift:
        w = jnp.concatenate([w[row_shift:, :], extra[0][...]], axis=0)
    w = w.astype(BF16)
    acc = _dot_nt(x_ref[...], w) if w_transposed else _dot(x_ref[...], w)
    if epilogue == "relu2":
        r = jnp.maximum(acc, 0.0)
        acc = r * r
    elif epilogue == "sigmoid":
        acc = _sigmoid(acc)
    elif epilogue == "resid":
        acc = acc + extra[-1][...]
    o_ref[...] = acc.astype(o_ref.dtype)


def _matmul(x, w, *, tm, tn=512, layer=None, n_cols=None, col_off=0, k_off=0, k_len=None, epilogue="none",
            resid=None, out_dtype=F32, w_transposed=False, row_shift=0):
    m, kdim = x.shape
    k_len = kdim if k_len is None else k_len
    n = w.shape[-2 if w_transposed else -1] if n_cols is None else n_cols
    tn = min(tn, n)
    assert m % tm == 0 and n % tn == 0 and k_off % k_len == 0 and col_off % tn == 0
    assert row_shift == 0 or (w_transposed and row_shift % 8 == 0 and tn % row_shift == 0)
    joff, kblk = col_off // tn, k_off // k_len
    x_mode = pl.Buffered(1) if (n // tn > 1 and tm * k_len * 2 > SINGLE_BUFFER_BYTES) else None

    def w_spec(rows, row_index):
        if w_transposed:
            blk, idx = (rows, k_len), lambda i, j: (row_index(j), kblk)
        else:
            blk, idx = (k_len, rows), lambda i, j: (kblk, row_index(j))
        if layer is None:
            return pl.BlockSpec(blk, idx)
        return pl.BlockSpec((None,) + blk, lambda i, j: (layer,) + idx(i, j))

    in_specs = [pl.BlockSpec((tm, k_len), lambda i, j: (i, kblk), pipeline_mode=x_mode),
                w_spec(tn, lambda j: j + joff)]
    args = [x, w]
    if row_shift:
        in_specs.append(w_spec(row_shift, lambda j: (j + joff + 1) * (tn // row_shift)))
        args.append(w)
    if epilogue == "resid":
        in_specs.append(pl.BlockSpec((tm, tn), lambda i, j: (i, j)))
        args.append(resid)
    return pl.pallas_call(
        functools.partial(_mm_kernel, epilogue=epilogue, w_transposed=w_transposed, row_shift=row_shift),
        grid=(m // tm, n // tn),
        in_specs=in_specs,
        out_specs=pl.BlockSpec((tm, tn), lambda i, j: (i, j)),
        out_shape=jax.ShapeDtypeStruct((m, n), out_dtype),
        compiler_params=_params(("parallel", "parallel")),
    )(*args)


def _merge_kernel(a_ref, b_ref, c_ref, wa_ref, wb_ref, wc_ref, ga_ref, gb_ref, gc_ref, o_ref):
    acc = ga_ref[...] * _dot(a_ref[...], wa_ref[...].astype(BF16))
    acc += gb_ref[...] * _dot(b_ref[...], wb_ref[...].astype(BF16))
    acc += gc_ref[...] * _dot(c_ref[...], wc_ref[...].astype(BF16))
    o_ref[...] = acc.astype(o_ref.dtype)


def _merge(a, b, c, wa, wb, wc, gates, *, layer, tm, tn=256):
    m, kdim = a.shape
    n = wa.shape[-1]
    assert m % tm == 0 and n % tn == 0 and gates.shape == (m, N_BRANCH * n)
    nj = n // tn
    x_mode = pl.Buffered(1) if tm * kdim * 2 > SINGLE_BUFFER_BYTES else None

    def x_spec():
        return pl.BlockSpec((tm, kdim), lambda i, j: (i, 0), pipeline_mode=x_mode)

    def g_spec(branch):
        return pl.BlockSpec((tm, tn), lambda i, j: (i, j + branch * nj))

    w_spec = pl.BlockSpec((None, kdim, tn), lambda i, j: (layer, 0, j))
    return pl.pallas_call(
        _merge_kernel,
        grid=(m // tm, nj),
        in_specs=[x_spec(), x_spec(), x_spec(), w_spec, w_spec, w_spec, g_spec(0), g_spec(1), g_spec(2)],
        out_specs=pl.BlockSpec((tm, tn), lambda i, j: (i, j)),
        out_shape=jax.ShapeDtypeStruct((m, n), BF16),
        compiler_params=_params(("parallel", "parallel")),
    )(a, b, c, wa, wb, wc, gates, gates, gates)


def _gmlp_kernel(u_ref, v_ref, w_ref, b_ref, g_ref, a_ref, *va_refs, chunk, chunks, gdim):
    row = lax.broadcasted_iota(jnp.int32, (chunk, chunk), 0)
    col = lax.broadcasted_iota(jnp.int32, (chunk, chunk), 1)
    causal = col <= row
    w_groups = [jnp.where(causal, w_ref[g], 0.0).astype(BF16) for g in range(A_GROUPS)]
    gain = g_ref[...]
    bias = b_ref[...]
    for c in range(chunks):
        rows = pl.ds(c * chunk, chunk)
        v = _gelu(v_ref[rows, :])
        vn = v * lax.rsqrt(jnp.mean(v * v, axis=-1, keepdims=True) + EPS) * gain
        if va_refs:
            va_refs[0][rows, :] = vn
        vb = vn.astype(BF16)
        for g in range(A_GROUPS):
            cols = slice(g * gdim, (g + 1) * gdim)
            mixed = _dot(w_groups[g], vb[:, cols]) + bias[:, g:g + 1]
            u = _gelu(u_ref[rows, cols])
            a_ref[rows, cols] = (u * mixed).astype(a_ref.dtype)


def _gmlp(proj, ws, bs_t, norm_g, *, width, chunk, chunks, row0, n_rows, out_rows, want_va):
    rows = chunk * chunks
    assert n_rows % rows == 0 and row0 % rows == 0
    blk0 = row0 // rows
    gdim = width // A_GROUPS
    out_shape = [jax.ShapeDtypeStruct((out_rows, width), BF16)]
    out_specs = [pl.BlockSpec((rows, width), lambda j: (j, 0))]
    if want_va:
        out_shape.append(jax.ShapeDtypeStruct((out_rows, width), F32))
        out_specs.append(pl.BlockSpec((rows, width), lambda j: (j, 0)))
    res = pl.pallas_call(
        functools.partial(_gmlp_kernel, chunk=chunk, chunks=chunks, gdim=gdim),
        grid=(n_rows // rows,),
        in_specs=[pl.BlockSpec((rows, width), lambda j: (blk0 + j, 0)),
                  pl.BlockSpec((rows, width), lambda j: (blk0 + j, 1)),
                  pl.BlockSpec((A_GROUPS, chunk, chunk), lambda j: (0, 0, 0)),
                  pl.BlockSpec((chunk, A_GROUPS), lambda j: (0, 0)),
                  pl.BlockSpec((1, width), lambda j: (0, 0))],
        out_specs=out_specs,
        out_shape=out_shape,
        compiler_params=_params(("parallel",)),
    )(proj, proj, ws, bs_t, norm_g.reshape(1, width))
    return res if want_va else (res[0], None)


def _qk_norm(x, g, scale):
    return x * lax.rsqrt(jnp.mean(x * x, axis=-1, keepdims=True) + EPS) * (g * scale)


ATTN_GROUP = 4


def _attn_prompt_kernel(*refs, seq, aliased):
    q_ref, k_ref, v_ref, qg_ref, kg_ref = refs[:5]
    o_ref, ko_ref, vo_ref, qp, kp, vp, os_, ls = refs[5 + (2 if aliased else 0):]
    v = v_ref[...]
    kn = _qk_norm(k_ref[...], kg_ref[...], 1.0)
    qn = _qk_norm(q_ref[...], qg_ref[...], B_HEAD_DIM ** -0.5)
    ko_ref[...] = kn
    vo_ref[...] = v
    data = pl.ds(B_BLOCK, seq)
    for p in range(len(B_PATTERNS)):
        kp[p, 0:B_BLOCK, :] = jnp.zeros((B_BLOCK, B_HEAD_DIM), BF16)
        vp[p, 0:B_BLOCK, :] = jnp.zeros((B_BLOCK, 2 * B_HEAD_DIM), BF16)
        vp[p, data, B_HEAD_DIM:] = jnp.ones((seq, B_HEAD_DIM), BF16)
    qp[0] = qn.astype(BF16)
    kp[0, data, :] = kn.astype(BF16)
    vp[0, data, 0:B_HEAD_DIM] = v.astype(BF16)
    os_[0] = qn
    for p, (win, dil) in enumerate(B_PATTERNS):
        if dil == 1:
            continue
        ln = seq // dil
        for r in range(dil):
            src = pl.ds(r, ln, stride=dil)
            qp[p, pl.ds(r * ln, ln), :] = os_[0, src, :].astype(BF16)
            kp[p, pl.ds(B_BLOCK + r * ln, ln), :] = ko_ref[src, :].astype(BF16)
            vp[p, pl.ds(B_BLOCK + r * ln, ln), 0:B_HEAD_DIM] = v_ref[src, :].astype(BF16)

    qi = lax.broadcasted_iota(jnp.int32, (B_BLOCK, B_BLOCK), 0)
    kj = lax.broadcasted_iota(jnp.int32, (B_BLOCK, B_BLOCK), 1)
    mask_cur = kj <= qi
    mask_prev = kj >= qi

    def group(p, dil, g0, firsts, position0):
        n_q = len(firsts) * B_BLOCK
        base = g0 * B_BLOCK
        if not isinstance(base, int):
            base = pl.multiple_of(base, B_BLOCK)
        s = _dot_nt(qp[p, pl.ds(base, n_q), :], kp[p, pl.ds(base, n_q + B_BLOCK), :])
        for u, first in enumerate(firsts):
            lo, mid, hi = u * B_BLOCK, (u + 1) * B_BLOCK, (u + 2) * B_BLOCK
            s_cur = jnp.where(mask_cur, s[lo:mid, mid:hi], NEG_INF)
            if first:
                m = jnp.max(s_cur, axis=-1, keepdims=True)
                res = _dot(jnp.exp(s_cur - m).astype(BF16), vp[p, pl.ds(base + mid, B_BLOCK), :])
            else:
                s_prev = jnp.where(mask_prev, s[lo:mid, lo:mid], NEG_INF)
                m = jnp.max(jnp.maximum(s_cur, s_prev), axis=-1, keepdims=True)
                probs = jnp.concatenate([jnp.exp(s_prev - m).astype(BF16), jnp.exp(s_cur - m).astype(BF16)], axis=1)
                res = _dot(probs, vp[p, pl.ds(base + lo, 2 * B_BLOCK), :])
            den = res[:, B_HEAD_DIM:]
            rows = pl.ds(position0(u), B_BLOCK, stride=dil) if dil > 1 else pl.ds(base + lo, B_BLOCK)
            os_[p, rows, :] = res[:, :B_HEAD_DIM] * (1.0 / den)
            ls[p, rows, :] = m + jnp.log(den)

    for p, (win, dil) in enumerate(B_PATTERNS):
        nb = seq // (dil * B_BLOCK)
        if nb == 1:
            def body(t, carry, p=p, dil=dil):
                group(p, dil, t * ATTN_GROUP, (True,) * ATTN_GROUP, lambda u: t * ATTN_GROUP + u)
                return carry

            lax.fori_loop(0, dil // ATTN_GROUP, body, 0, unroll=True)
            continue
        head = (True,) + (False,) * (ATTN_GROUP - 1)
        tail = (False,) * ATTN_GROUP
        per_class = nb // ATTN_GROUP

        def one_class(r, carry, p=p, dil=dil, nb=nb, per_class=per_class, head=head, tail=tail):
            group(p, dil, r * nb, head, lambda u: u * (B_BLOCK * dil) + r)
            if per_class > 1:
                def rest(t, c2):
                    group(p, dil, r * nb + t * ATTN_GROUP, tail,
                          lambda u: (t * ATTN_GROUP + u) * (B_BLOCK * dil) + r)
                    return c2

                lax.fori_loop(1, per_class, rest, 0, unroll=True)
            return carry

        if dil == 1:
            one_class(0, 0)
        else:
            lax.fori_loop(0, dil, one_class, 0, unroll=True)

    l0, l1, l2 = ls[0], ls[1], ls[2]
    mx = jnp.maximum(jnp.maximum(l0, l1), l2)
    e0, e1, e2 = jnp.exp(l0 - mx), jnp.exp(l1 - mx), jnp.exp(l2 - mx)
    out = (e0 * os_[0] + e1 * os_[1] + e2 * os_[2]) * (1.0 / (e0 + e1 + e2))
    o_ref[...] = out.astype(o_ref.dtype)


def _attn_prompt(proj, qg, kg, *, n_seq, seq, col0, out_rows, layer, depth, kv_prev=None):
    assert B_PATTERNS[0][1] == 1 and len(B_PATTERNS) == 3
    for _, dil in B_PATTERNS:
        nb = seq // (dil * B_BLOCK)
        assert seq % (dil * B_BLOCK) == 0 and (nb % ATTN_GROUP == 0 or (nb == 1 and dil % ATTN_GROUP == 0))
    width = B_HEADS * B_HEAD_DIM
    blk = (seq, B_HEAD_DIM)

    def in_spec(part):
        return pl.BlockSpec(blk, lambda i, h: (i, col0 + part * B_HEADS + h))

    g_spec = pl.BlockSpec((1, B_HEAD_DIM), lambda i, h: (0, 0))
    kv_spec = pl.BlockSpec((None, None, seq, B_HEAD_DIM), lambda i, h: (layer, i, 0, h))
    kv_shape = jax.ShapeDtypeStruct((depth, n_seq, seq, width), F32)
    in_specs = [in_spec(0), in_spec(1), in_spec(2), g_spec, g_spec]
    args = [proj, proj, proj, qg.reshape(1, -1), kg.reshape(1, -1)]
    aliases = {}
    if kv_prev is not None:
        in_specs += [pl.BlockSpec(memory_space=pl.ANY)] * 2
        args += list(kv_prev)
        aliases = {5: 1, 6: 2}
    n_pat = len(B_PATTERNS)
    return pl.pallas_call(
        functools.partial(_attn_prompt_kernel, seq=seq, aliased=kv_prev is not None),
        grid=(n_seq, B_HEADS),
        in_specs=in_specs,
        out_specs=[pl.BlockSpec(blk, lambda i, h: (i, h)), kv_spec, kv_spec],
        out_shape=[jax.ShapeDtypeStruct((out_rows, width), BF16), kv_shape, kv_shape],
        input_output_aliases=aliases,
        scratch_shapes=[pltpu.VMEM((n_pat, seq, B_HEAD_DIM), BF16),
                        pltpu.VMEM((n_pat, seq + B_BLOCK, B_HEAD_DIM), BF16),
                        pltpu.VMEM((n_pat, seq + B_BLOCK, 2 * B_HEAD_DIM), BF16)]
        + [pltpu.VMEM((n_pat, seq, B_HEAD_DIM), F32)] * 2,
        compiler_params=_params(("parallel", "parallel")),
    )(*args)


SAMPLE_HEADS_PER_STEP = 8


def _divmod_const(x, d):
    if d & (d - 1) == 0:
        return x >> (d.bit_length() - 1), x & (d - 1)
    return x // d, x % d


def _attn_sample_kernel(q_ref, k_ref, v_ref, kc_ref, vc_ref, qg_ref, kg_ref, o_ref, ko_ref, vo_ref,
                        qs, ks, vs, *, t_new, buf):
    hs = SAMPLE_HEADS_PER_STEP
    n_rows = hs * t_new
    for hh in range(hs):
        cols = slice(hh * B_HEAD_DIM, (hh + 1) * B_HEAD_DIM)
        rows = slice(hh * t_new, (hh + 1) * t_new)
        kn = _qk_norm(k_ref[0, :, cols], kg_ref[...], 1.0)
        ko_ref[0, :, cols] = kn
        vo_ref[0, :, cols] = v_ref[0, :, cols]
        qs[rows, :] = _qk_norm(q_ref[0, :, cols], qg_ref[...], B_HEAD_DIM ** -0.5)
        ks[rows, :] = kn
        vs[rows, :] = v_ref[0, :, cols]

    qb = qs[...].astype(BF16)
    s_c = _dot_nt(qb, kc_ref[...].reshape(buf * hs, B_HEAD_DIM).astype(BF16))
    s_n = _dot_nt(qb, ks[...].astype(BF16))
    vcb = vc_ref[...].reshape(buf * hs, B_HEAD_DIM).astype(BF16)
    vnew = vs[...].astype(BF16)

    row_c = lax.broadcasted_iota(jnp.int32, (n_rows, buf * hs), 0)
    col_c = lax.broadcasted_iota(jnp.int32, (n_rows, buf * hs), 1)
    head_r, tok_r = _divmod_const(row_c, t_new)
    cache_row, head_c = _divmod_const(col_c, hs)
    same_c = head_r == head_c
    dist_c = buf + tok_r - cache_row
    row_n = lax.broadcasted_iota(jnp.int32, (n_rows, n_rows), 0)
    col_n = lax.broadcasted_iota(jnp.int32, (n_rows, n_rows), 1)
    head_rn, tok_rn = _divmod_const(row_n, t_new)
    head_cn, tok_cn = _divmod_const(col_n, t_new)
    dist_n = tok_rn - tok_cn
    same_n = jnp.logical_and(head_rn == head_cn, dist_n >= 0)

    outs, lses = [], []
    for win, dil in B_PATTERNS:
        ok_c = jnp.logical_and(same_c, jnp.logical_and((dist_c & (dil - 1)) == 0, dist_c <= win))
        ok_n = jnp.logical_and(same_n, (dist_n & (dil - 1)) == 0)
        sc = jnp.where(ok_c, s_c, NEG_INF)
        sn = jnp.where(ok_n, s_n, NEG_INF)
        m = jnp.maximum(jnp.max(sc, axis=-1, keepdims=True), jnp.max(sn, axis=-1, keepdims=True))
        pc = jnp.exp(sc - m)
        pn = jnp.exp(sn - m)
        den = jnp.sum(pc, axis=-1, keepdims=True) + jnp.sum(pn, axis=-1, keepdims=True)
        num = _dot(pc.astype(BF16), vcb) + _dot(pn.astype(BF16), vnew)
        outs.append(num * (1.0 / den))
        lses.append(m + jnp.log(den))
    mx = jnp.maximum(jnp.maximum(lses[0], lses[1]), lses[2])
    es = [jnp.exp(l - mx) for l in lses]
    out = (es[0] * outs[0] + es[1] * outs[1] + es[2] * outs[2]) * (1.0 / (es[0] + es[1] + es[2]))
    for hh in range(hs):
        o_ref[0, :, hh * B_HEAD_DIM:(hh + 1) * B_HEAD_DIM] = out[hh * t_new:(hh + 1) * t_new, :]


def _attn_sample(qkv, cache_k, cache_v, qg, kg, *, layer):
    n, t_new, _ = qkv.shape
    buf = cache_k.shape[2]
    assert cache_k.shape[3:] == (B_HEADS, B_HEAD_DIM)
    assert all(win <= buf for win, _ in B_PATTERNS) and all(dil & (dil - 1) == 0 for _, dil in B_PATTERNS)
    assert (SAMPLE_HEADS_PER_STEP * t_new) % 8 == 0
    width = B_HEADS * B_HEAD_DIM
    hw = SAMPLE_HEADS_PER_STEP * B_HEAD_DIM
    steps = width // hw
    blk = (1, t_new, hw)

    def in_spec(part):
        return pl.BlockSpec(blk, lambda i, h: (i, 0, part * steps + h))

    c_spec = pl.BlockSpec((None, None, buf, SAMPLE_HEADS_PER_STEP, B_HEAD_DIM), lambda i, h: (layer, i, 0, h, 0))
    g_spec = pl.BlockSpec((1, B_HEAD_DIM), lambda i, h: (0, 0))
    out_spec = pl.BlockSpec(blk, lambda i, h: (i, 0, h))
    return pl.pallas_call(
        functools.partial(_attn_sample_kernel, t_new=t_new, buf=buf),
        grid=(n, steps),
        in_specs=[in_spec(0), in_spec(1), in_spec(2), c_spec, c_spec, g_spec, g_spec],
        out_specs=[out_spec, out_spec, out_spec],
        out_shape=[jax.ShapeDtypeStruct((n, t_new, width), F32)] * 3,
        scratch_shapes=[pltpu.VMEM((SAMPLE_HEADS_PER_STEP * t_new, B_HEAD_DIM), F32)] * 3,
        compiler_params=_params(("parallel", "parallel")),
    )(qkv, qkv, qkv, cache_k, cache_v, qg.reshape(1, -1), kg.reshape(1, -1))


def _split3(x):
    hi = x.astype(BF16)
    r1 = x - hi.astype(F32)
    mid = r1.astype(BF16)
    lo = (r1 - mid.astype(F32)).astype(BF16)
    return hi, mid, lo


def _mlstm_kernel(q_ref, k_ref, v_ref, o_ref, ifc_ref, ifr_ref, bc_ref, br_ref, hg_ref, c0_ref, n0_ref, m0_ref,
                  h_ref, cf_ref, nf_ref, mf_ref, ct_s, n_s, m_s, *, ln):
    j = pl.program_id(1)
    last = pl.num_programs(1) - 1

    @pl.when(j == 0)
    def _():
        for h in range(C_HEADS):
            ct_s[h] = c0_ref[0, h].T
        n_s[...] = n0_ref[0]
        m_s[...] = m0_ref[0]

    ifc = ifc_ref[...] + bc_ref[...]
    ifr = ifr_ref[0] + br_ref[...]
    row = lax.broadcasted_iota(jnp.int32, (ln, ln), 0)
    col = lax.broadcasted_iota(jnp.int32, (ln, ln), 1)
    causal = col <= row
    tri = jnp.where(causal, 1.0, 0.0).astype(BF16)
    triu = jnp.where(row <= col, 1.0, 0.0).astype(BF16)
    b_col = sum(_dot(tri, part) for part in _split3(_log_sigmoid(ifc)))
    b_row = sum(_dot(part, triu) for part in _split3(_log_sigmoid(ifr)))

    for h in range(C_HEADS):
        qc = q_ref[:, h * C_QK_DIM:(h + 1) * C_QK_DIM]
        qb = qc.astype(BF16)
        kc = k_ref[:, h * C_QK_DIM:(h + 1) * C_QK_DIM] * (C_QK_DIM ** -0.5)
        kb = kc.astype(BF16)
        vc = v_ref[:, h * C_V_DIM:(h + 1) * C_V_DIM]
        bcol = b_col[:, C_HEADS + h:C_HEADS + h + 1]
        brow = b_row[C_HEADS + h:C_HEADS + h + 1, :]
        icol = ifc[:, h:h + 1]
        irow = ifr[h:h + 1, :]
        m_prev = m_s[h:h + 1, 0:1]
        n_prev = n_s[h:h + 1, :]
        ct_prev = ct_s[h]

        dmat = jnp.where(causal, bcol - brow + irow, NEG_INF)
        inter = bcol + m_prev
        mt = jnp.maximum(inter, jnp.max(dmat, axis=-1, keepdims=True))
        wmat = jnp.exp(dmat - mt) * _dot_nt(qb, kb)
        w_inter = jnp.exp(inter - mt)
        num = _dot(wmat.astype(BF16), vc.astype(BF16)) + w_inter * _dot(qb, ct_prev.astype(BF16))
        den = jnp.sum(wmat, axis=-1, keepdims=True) + w_inter * jnp.sum(qc * n_prev, axis=-1, keepdims=True)
        hc = num * (1.0 / jnp.maximum(jnp.abs(den), jnp.exp(-mt)))

        m_new = mt[ln - 1:ln, :]
        b_last = bcol[ln - 1:ln, :]
        w_end = jnp.exp(b_last - bcol + icol - m_new)
        decay = jnp.exp(b_last + m_prev - m_new)
        ct_s[h] = decay * ct_prev + _dot(kc.T.astype(BF16), (w_end * vc).astype(BF16))
        n_s[h:h + 1, :] = decay * n_prev + jnp.sum(w_end * kc, axis=0, keepdims=True)
        m_s[h:h + 1, :] = jnp.broadcast_to(m_new, (1, LANES))

        gain = hg_ref[:, h * C_V_DIM:(h + 1) * C_V_DIM]
        hn = hc * lax.rsqrt(jnp.mean(hc * hc, axis=-1, keepdims=True) + EPS) * gain
        og = _sigmoid(o_ref[:, h * C_V_DIM:(h + 1) * C_V_DIM])
        h_ref[:, h * C_V_DIM:(h + 1) * C_V_DIM] = (og * hn).astype(h_ref.dtype)

    @pl.when(j == last)
    def _():
        for h in range(C_HEADS):
            cf_ref[0, h] = ct_s[h].T
        nf_ref[0] = n_s[...]
        mf_ref[0] = m_s[...]


def _mlstm(proj, ifc, ifr, bias_c, bias_r, hn_g, c0, n0, m0, *, n_seq, seq, col_q, col_k, col_v, col_o, ln,
           out_rows):
    ln = min(ln, seq)
    nc = seq // ln
    assert seq % ln == 0 and ln % LANES == 0
    qk_w = C_HEADS * C_QK_DIM
    v_w = C_HEADS * C_V_DIM
    assert col_q % qk_w == 0 and col_k % qk_w == 0 and col_v % v_w == 0 and col_o % v_w == 0
    state_specs = [pl.BlockSpec((1, C_HEADS, C_V_DIM, C_QK_DIM), lambda i, j: (i, 0, 0, 0)),
                   pl.BlockSpec((1, C_HEADS, C_QK_DIM), lambda i, j: (i, 0, 0)),
                   pl.BlockSpec((1, C_HEADS, LANES), lambda i, j: (i, 0, 0))]
    return pl.pallas_call(
        functools.partial(_mlstm_kernel, ln=ln),
        grid=(n_seq, nc),
        in_specs=[pl.BlockSpec((ln, qk_w), lambda i, j: (i * nc + j, col_q // qk_w)),
                  pl.BlockSpec((ln, qk_w), lambda i, j: (i * nc + j, col_k // qk_w)),
                  pl.BlockSpec((ln, v_w), lambda i, j: (i * nc + j, col_v // v_w)),
                  pl.BlockSpec((ln, v_w), lambda i, j: (i * nc + j, col_o // v_w)),
                  pl.BlockSpec((ln, LANES), lambda i, j: (i * nc + j, 0)),
                  pl.BlockSpec((1, 2 * C_HEADS, ln), lambda i, j: (i, 0, j)),
                  pl.BlockSpec((1, LANES), lambda i, j: (0, 0)),
                  pl.BlockSpec((2 * C_HEADS, 1), lambda i, j: (0, 0)),
                  pl.BlockSpec((1, v_w), lambda i, j: (0, 0))] + state_specs,
        out_specs=[pl.BlockSpec((ln, v_w), lambda i, j: (i * nc + j, 0))] + state_specs,
        out_shape=[jax.ShapeDtypeStruct((out_rows, v_w), BF16),
                   jax.ShapeDtypeStruct((n_seq, C_HEADS, C_V_DIM, C_QK_DIM), F32),
                   jax.ShapeDtypeStruct((n_seq, C_HEADS, C_QK_DIM), F32),
                   jax.ShapeDtypeStruct((n_seq, C_HEADS, LANES), F32)],
        scratch_shapes=[pltpu.VMEM((C_HEADS, C_QK_DIM, C_V_DIM), F32),
                        pltpu.VMEM((C_HEADS, C_QK_DIM), F32),
                        pltpu.VMEM((C_HEADS, LANES), F32)],
        compiler_params=_params(("parallel", "arbitrary")),
    )(proj, proj, proj, proj, ifc, ifr, bias_c, bias_r, hn_g.reshape(1, v_w), c0, n0, m0)


MLSTM_CHUNK = 256
GATE_PAD = 1e30
ROW_TILES = 4
MERGE_ROW_TILES = 6
FF2_SPLITS = 2


def _put_rows(full, rows, row0, n_pad):
    block = jnp.pad(rows.astype(full.dtype), ((0, n_pad - rows.shape[0]), (0, 0)))
    return lax.dynamic_update_slice(full, block, (row0, 0))


def kernel(x_prompt, x_sample, cache_swa_k, cache_swa_v, state_mlstm_C, state_mlstm_n, state_mlstm_m, norm1_g, w_in, ws_a, bs_a, norm_va_g, qn_g, kn_g, i_b, f_b, hn_c_g, w_branch_a, w_branch_b, w_branch_c, w_out, norm2_g, w_ff1, w_ff2):
    depth = w_in.shape[0]
    nbp, seq, d = x_prompt.shape
    nbs, t_new, _ = x_sample.shape
    buf = cache_swa_k.shape[2]
    hd = B_HEADS * B_HEAD_DIM
    a_width = d // 2
    qk_w = C_HEADS * C_QK_DIM
    v_w = C_HEADS * C_V_DIM
    n_gate = N_BRANCH * d
    n_main = w_in.shape[2] - n_gate - IF_COLS
    col_b = 2 * a_width
    col_c = col_b + 3 * hd
    assert col_c + 2 * qk_w + 2 * v_w == n_main and seq % A_CHUNK == 0 and t_new <= A_CHUNK

    mp, ms = nbp * seq, nbs * t_new
    quantum = BF16_SUBLANES * math.lcm(ROW_TILES, MERGE_ROW_TILES)
    m_tot = -(-(mp + ms) // quantum) * quantum
    n_tail = m_tot - mp
    tm = m_tot // ROW_TILES
    tm_merge = m_tot // MERGE_ROW_TILES
    tm_norm = m_tot // (ROW_TILES * 3)
    assert mp % n_tail == 0 and n_tail % t_new == 0 and tm_norm % BF16_SUBLANES == 0
    x = jnp.concatenate([x_prompt.reshape(mp, d), x_sample.reshape(ms, d), jnp.zeros((m_tot - mp - ms, d), F32)])

    w_in_t = jnp.swapaxes(w_in, 1, 2)
    zero_state = (jnp.zeros((nbp, C_HEADS, C_V_DIM, C_QK_DIM), F32), jnp.zeros((nbp, C_HEADS, C_QK_DIM), F32),
                  jnp.zeros((nbp, C_HEADS, LANES), F32))
    lane = jnp.arange(LANES)
    gate_pad_row = jnp.where(lane < C_HEADS, -GATE_PAD, jnp.where(lane < IF_COLS, GATE_PAD, 0.0)).astype(F32)
    seqs_tail = n_tail // t_new

    kv_prompt = None
    small = [[] for _ in range(9)]
    for l in range(depth):
        w_if = jnp.pad(w_in_t[l, n_main:n_main + IF_COLS, :], ((0, LANES - IF_COLS), (0, 0)))
        bias = jnp.concatenate([i_b[l], f_b[l]])
        bias_c = jnp.pad(bias, (0, LANES - IF_COLS)).reshape(1, LANES)
        bias_r = bias.reshape(IF_COLS, 1)

        xn = _rmsnorm_bf16(x, norm1_g[l], tm=tm_norm)
        proj = _matmul(xn, w_in_t, layer=l, n_cols=n_main, tm=tm, w_transposed=True)
        ifp = _matmul(xn, w_if, tm=tm, w_transposed=True)
        gates = _matmul(xn, w_in_t, layer=l, n_cols=n_gate, col_off=n_main, row_shift=IF_COLS, tm=tm,
                        w_transposed=True, epilogue="sigmoid")

        a_out, _ = _gmlp(proj, ws_a[l], bs_a[l].T, norm_va_g[l], width=a_width, chunk=A_CHUNK, chunks=2,
                         row0=0, n_rows=mp, out_rows=m_tot, want_va=False)
        eye = jnp.eye(seqs_tail, dtype=F32)
        ws_tail = jax.vmap(lambda w: jnp.kron(eye, w[:t_new, :t_new]))(ws_a[l])
        bs_tail = jnp.tile(bs_a[l][:, :t_new], (1, seqs_tail)).T
        a_tail, va_tail = _gmlp(proj, ws_tail, bs_tail, norm_va_g[l], width=a_width, chunk=n_tail, chunks=1,
                                row0=mp, n_rows=n_tail, out_rows=n_tail, want_va=True)
        a_out = lax.dynamic_update_slice(a_out, a_tail, (mp, 0))

        b_out, kp, vp = _attn_prompt(proj, qn_g[l], kn_g[l], n_seq=nbp, seq=seq, col0=col_b // B_HEAD_DIM,
                                     out_rows=m_tot, layer=l, depth=depth, kv_prev=kv_prompt)
        kv_prompt = (kp, vp)
        qkv_s = proj[mp:mp + ms, col_b:col_c].reshape(nbs, t_new, 3 * hd)
        b_s, k_s, v_s = _attn_sample(qkv_s, cache_swa_k, cache_swa_v, qn_g[l], kn_g[l], layer=l)
        b_out = _put_rows(b_out, b_s.reshape(ms, hd), mp, n_tail)

        cols = dict(col_q=col_c, col_k=col_c + qk_w, col_v=col_c + 2 * qk_w, col_o=col_c + 2 * qk_w + v_w)
        ifr = jnp.swapaxes(ifp[:mp, :IF_COLS].reshape(nbp, seq, IF_COLS), 1, 2)
        c_out, cp, n_p, m_p = _mlstm(proj, ifp, ifr, bias_c, bias_r, hn_c_g[l], *zero_state, n_seq=nbp, seq=seq,
                                     ln=MLSTM_CHUNK, out_rows=m_tot, **cols)
        pad = A_CHUNK - t_new
        proj_s = jnp.pad(proj[mp:mp + ms, col_c:].reshape(nbs, t_new, n_main - col_c), ((0, 0), (0, pad), (0, 0)))
        ifc_s = jnp.concatenate([ifp[mp:mp + ms].reshape(nbs, t_new, LANES),
                                 jnp.broadcast_to(gate_pad_row, (nbs, pad, LANES))], axis=1)
        ifr_s = jnp.swapaxes(ifc_s[:, :, :IF_COLS], 1, 2)
        m0 = jnp.broadcast_to(state_mlstm_m[l][..., None], (nbs, C_HEADS, LANES))
        c_s_out, c_s, n_s, m_s = _mlstm(
            proj_s.reshape(nbs * A_CHUNK, -1), ifc_s.reshape(nbs * A_CHUNK, LANES), ifr_s, bias_c, bias_r, hn_c_g[l],
            state_mlstm_C[l], state_mlstm_n[l], m0, n_seq=nbs, seq=A_CHUNK, ln=A_CHUNK, out_rows=nbs * A_CHUNK,
            col_q=0, col_k=qk_w, col_v=2 * qk_w, col_o=2 * qk_w + v_w)
        c_out = _put_rows(c_out, c_s_out.reshape(nbs, A_CHUNK, v_w)[:, :t_new].reshape(ms, v_w), mp, n_tail)

        merged = _merge(a_out, b_out, c_out, w_branch_a, w_branch_b, w_branch_c, gates, layer=l, tm=tm_merge)
        x = _matmul(merged, w_out, layer=l, epilogue="resid", resid=x, tm=tm, tn=256)
        h = _matmul(_rmsnorm_bf16(x, norm2_g[l], tm=tm_norm), w_ff1, layer=l, epilogue="relu2", out_dtype=BF16, tm=tm)
        k_len = w_ff2.shape[1] // FF2_SPLITS
        for part in range(FF2_SPLITS):
            x = _matmul(h, w_ff2, layer=l, k_off=part * k_len, k_len=k_len, epilogue="resid", resid=x,
                        tm=tm_merge, tn=256)

        vals = (cp, n_p, m_p[..., 0], k_s.reshape(nbs, t_new, B_HEADS, B_HEAD_DIM),
                v_s.reshape(nbs, t_new, B_HEADS, B_HEAD_DIM), c_s, n_s, m_s[..., 0],
                va_tail[:ms].reshape(nbs, t_new, a_width))
        for o, v in zip(small, vals):
            o.append(v)

    keep = min(max(w for w, _ in B_PATTERNS), seq)
    kp, vp = (t[:, :, seq - keep:].reshape(depth, nbp, keep, B_HEADS, B_HEAD_DIM) for t in kv_prompt)
    st = [jnp.stack(o) for o in small]
    return (x[:mp].reshape(nbp, seq, d), x[mp:mp + ms].reshape(nbs, t_new, d), kp, vp, st[0], st[1], st[2],
            st[3], st[4], st[5], st[6], st[7], st[8])
```

```python
import functools
import math

import jax
import jax.numpy as jnp
from jax import lax
from jax.experimental import pallas as pl
from jax.experimental.pallas import tpu as pltpu

F32 = jnp.float32
BF16 = jnp.bfloat16

EPS = 1e-6
NEG_INF = -1e30

A_CHUNK = 128
A_GROUPS = 8
B_HEADS = 16
B_HEAD_DIM = 128
B_PATTERNS = ((128, 1), (512, 4), (2048, 16))
B_BLOCK = 128
C_HEADS = 8
C_QK_DIM = 128
C_V_DIM = 256
N_BRANCH = 3
IF_COLS = 2 * C_HEADS

V7X_VMEM_BYTES = 64 * 1024 * 1024
VMEM_LIMIT_BYTES = V7X_VMEM_BYTES - 8 * 1024 * 1024
MERGE_VMEM_LIMIT_BYTES = V7X_VMEM_BYTES - 3 * 1024 * 1024
LANES = 128
BF16_SUBLANES = 16
SINGLE_BUFFER_BYTES = 4 << 20


def _params(semantics, vmem_limit_bytes=VMEM_LIMIT_BYTES):
    return pltpu.CompilerParams(dimension_semantics=semantics, vmem_limit_bytes=vmem_limit_bytes)


def _gelu(x):
    c = math.sqrt(2.0 / math.pi)
    return 0.5 * x * (1.0 + jnp.tanh(c * (x + 0.044715 * (x * x * x))))


def _sigmoid(x):
    return 1.0 / (1.0 + jnp.exp(-x))


def _log_sigmoid(x):
    return -(jnp.maximum(-x, 0.0) + jnp.log(1.0 + jnp.exp(-jnp.abs(x))))


def _dot(a, b):
    return jnp.dot(a, b, preferred_element_type=F32)


def _dot_nt(a, b):
    return lax.dot_general(a, b, (((1,), (1,)), ((), ())), preferred_element_type=F32)


def _rmsnorm_kernel(x_ref, g_ref, o_ref):
    x = x_ref[...]
    y = x * lax.rsqrt(jnp.mean(x * x, axis=-1, keepdims=True) + EPS)
    o_ref[...] = (y * g_ref[...]).astype(o_ref.dtype)


def _rmsnorm_bf16(x, g, *, tm):
    m, d = x.shape
    assert m % tm == 0
    return pl.pallas_call(
        _rmsnorm_kernel,
        grid=(m // tm,),
        in_specs=[pl.BlockSpec((tm, d), lambda i: (i, 0)), pl.BlockSpec((1, d), lambda i: (0, 0))],
        out_specs=pl.BlockSpec((tm, d), lambda i: (i, 0)),
        out_shape=jax.ShapeDtypeStruct((m, d), BF16),
        compiler_params=_params(("parallel",)),
    )(x, g.reshape(1, d))


def _mm_kernel(x_ref, w_ref, *rest, epilogue, w_transposed, tail):
    n_out = 2 if tail else 1
    outs = rest[-n_out:]
    extra = rest[:-n_out]
    w = (w_ref[0] if len(w_ref.shape) == 3 else w_ref[...]).astype(BF16)
    acc = _dot_nt(x_ref[...], w) if w_transposed else _dot(x_ref[...], w)
    if epilogue == "relu2":
        r = jnp.maximum(acc, 0.0)
        acc = r * r
    elif epilogue == "resid":
        acc = acc + extra[-1][...]
    outs[0][...] = acc.astype(outs[0].dtype)
    if tail:
        lo, n_tail = tail
        outs[1][...] = acc[lo:lo + n_tail, :].astype(outs[1].dtype)


def _matmul(x, w, *, tm, tn=512, layer=None, n_cols=None, col_off=0, k_off=0, k_len=None, epilogue="none",
            resid=None, out_dtype=F32, w_transposed=False, row_shift=0, main_rows=None):
    m, kdim = x.shape
    k_len = kdim if k_len is None else k_len
    n = w.shape[-2 if w_transposed else -1] if n_cols is None else n_cols
    tn = min(tn, n)
    assert m % tm == 0 and n % tn == 0 and k_off % k_len == 0 and col_off % tn == 0
    assert row_shift == 0 or (w_transposed and row_shift % 8 == 0)
    joff, kblk = col_off // tn, k_off // k_len
    x_mode = pl.Buffered(1) if (n // tn > 1 and tm * k_len * 2 > SINGLE_BUFFER_BYTES) else None

    if row_shift:
        assert layer is not None
        w_spec = pl.BlockSpec((pl.Element(1), pl.Element(tn), pl.Element(k_len)),
                              lambda i, j: (layer, ((col_off + row_shift) // 8 + j * (tn // 8)) * 8, k_off))
    else:
        if w_transposed:
            w_blk, w_idx = (tn, k_len), lambda i, j: (j + joff, kblk)
        else:
            w_blk, w_idx = (k_len, tn), lambda i, j: (kblk, j + joff)
        if layer is None:
            w_spec = pl.BlockSpec(w_blk, w_idx)
        else:
            w_spec = pl.BlockSpec((None,) + w_blk, lambda i, j: (layer,) + w_idx(i, j))
    in_specs = [pl.BlockSpec((tm, k_len), lambda i, j: (i, kblk), pipeline_mode=x_mode), w_spec]
    args = [x, w]
    if epilogue == "resid":
        in_specs.append(pl.BlockSpec((tm, tn), lambda i, j: (i, j)))
        args.append(resid)
    out_specs = pl.BlockSpec((tm, tn), lambda i, j: (i, j))
    out_shape = jax.ShapeDtypeStruct((m, n), out_dtype)
    tail = None
    if main_rows is not None:
        n_tail = m - main_rows
        lo = main_rows - (m // tm - 1) * tm
        assert 0 <= lo and lo % 8 == 0 and lo + n_tail == tm
        tail = (lo, n_tail)
        out_specs = [out_specs, pl.BlockSpec((n_tail, tn), lambda i, j: (i, j))]
        out_shape = [jax.ShapeDtypeStruct((main_rows, n), out_dtype),
                     jax.ShapeDtypeStruct((m // tm * n_tail, n), out_dtype)]
    res = pl.pallas_call(
        functools.partial(_mm_kernel, epilogue=epilogue, w_transposed=w_transposed, tail=tail),
        grid=(m // tm, n // tn),
        in_specs=in_specs,
        out_specs=out_specs,
        out_shape=out_shape,
        compiler_params=_params(("parallel", "parallel")),
    )(*args)
    if tail:
        return res[0], res[1][(m // tm - 1) * tail[1]:]
    return res


def _merge_kernel(a_ref, b_ref, c_ref, wa_ref, wb_ref, wc_ref, ga_ref, gb_ref, gc_ref, o_ref):
    acc = _sigmoid(ga_ref[...]) * _dot(a_ref[...], wa_ref[...].astype(BF16))
    acc += _sigmoid(gb_ref[...]) * _dot(b_ref[...], wb_ref[...].astype(BF16))
    acc += _sigmoid(gc_ref[...]) * _dot(c_ref[...], wc_ref[...].astype(BF16))
    o_ref[...] = acc.astype(o_ref.dtype)


def _merge(a, b, c, wa, wb, wc, gates, *, layer, tm, tn=256):
    m, kdim = a.shape
    n = wa.shape[-1]
    assert m % tm == 0 and n % tn == 0 and gates.shape == (m, N_BRANCH * n)
    nj = n // tn
    x_mode = pl.Buffered(1) if tm * kdim * 2 > SINGLE_BUFFER_BYTES else None

    def x_spec():
        return pl.BlockSpec((tm, kdim), lambda i, j: (i, 0), pipeline_mode=x_mode)

    def g_spec(branch):
        return pl.BlockSpec((tm, tn), lambda i, j: (i, j + branch * nj))

    w_spec = pl.BlockSpec((None, kdim, tn), lambda i, j: (layer, 0, j))
    return pl.pallas_call(
        _merge_kernel,
        grid=(m // tm, nj),
        in_specs=[x_spec(), x_spec(), x_spec(), w_spec, w_spec, w_spec, g_spec(0), g_spec(1), g_spec(2)],
        out_specs=pl.BlockSpec((tm, tn), lambda i, j: (i, j)),
        out_shape=jax.ShapeDtypeStruct((m, n), BF16),
        compiler_params=_params(("parallel", "parallel"), MERGE_VMEM_LIMIT_BYTES),
    )(a, b, c, wa, wb, wc, gates, gates, gates)


def _gmlp_kernel(u_ref, v_ref, w_ref, b_ref, g_ref, a_ref, *va_refs, chunk, chunks, gdim):
    row = lax.broadcasted_iota(jnp.int32, (chunk, chunk), 0)
    col = lax.broadcasted_iota(jnp.int32, (chunk, chunk), 1)
    causal = col <= row
    w_groups = [jnp.where(causal, w_ref[g], 0.0).astype(BF16) for g in range(A_GROUPS)]
    gain = g_ref[...]
    bias = b_ref[...]
    for c in range(chunks):
        rows = pl.ds(c * chunk, chunk)
        v = _gelu(v_ref[rows, :])
        vn = v * lax.rsqrt(jnp.mean(v * v, axis=-1, keepdims=True) + EPS) * gain
        if va_refs:
            va_refs[0][rows, :] = vn
        vb = vn.astype(BF16)
        for g in range(A_GROUPS):
            cols = slice(g * gdim, (g + 1) * gdim)
            mixed = _dot(w_groups[g], vb[:, cols]) + bias[:, g:g + 1]
            u = _gelu(u_ref[rows, cols])
            a_ref[rows, cols] = (u * mixed).astype(a_ref.dtype)


def _gmlp(proj, ws, bs_t, norm_g, *, width, chunk, chunks, row0, n_rows, out_rows, want_va):
    rows = chunk * chunks
    assert n_rows % rows == 0 and row0 % rows == 0
    blk0 = row0 // rows
    gdim = width // A_GROUPS
    out_shape = [jax.ShapeDtypeStruct((out_rows, width), BF16)]
    out_specs = [pl.BlockSpec((rows, width), lambda j: (j, 0))]
    if want_va:
        out_shape.append(jax.ShapeDtypeStruct((out_rows, width), F32))
        out_specs.append(pl.BlockSpec((rows, width), lambda j: (j, 0)))
    res = pl.pallas_call(
        functools.partial(_gmlp_kernel, chunk=chunk, chunks=chunks, gdim=gdim),
        grid=(n_rows // rows,),
        in_specs=[pl.BlockSpec((rows, width), lambda j: (blk0 + j, 0)),
                  pl.BlockSpec((rows, width), lambda j: (blk0 + j, 1)),
                  pl.BlockSpec((A_GROUPS, chunk, chunk), lambda j: (0, 0, 0)),
                  pl.BlockSpec((chunk, A_GROUPS), lambda j: (0, 0)),
                  pl.BlockSpec((1, width), lambda j: (0, 0))],
        out_specs=out_specs,
        out_shape=out_shape,
        compiler_params=_params(("parallel",)),
    )(proj, proj, ws, bs_t, norm_g.reshape(1, width))
    return res if want_va else (res[0], None)


def _qk_norm(x, g, scale):
    return x * lax.rsqrt(jnp.mean(x * x, axis=-1, keepdims=True) + EPS) * (g * scale)


ATTN_GROUP = 4


def _attn_prompt_kernel(*refs, seq, aliased):
    q_ref, k_ref, v_ref, qg_ref, kg_ref = refs[:5]
    o_ref, ko_ref, vo_ref, qp, kp, vp, os_, ls = refs[5 + (2 if aliased else 0):]
    v = v_ref[...]
    kn = _qk_norm(k_ref[...], kg_ref[...], 1.0)
    qn = _qk_norm(q_ref[...], qg_ref[...], B_HEAD_DIM ** -0.5)
    ko_ref[...] = kn
    vo_ref[...] = v
    data = pl.ds(B_BLOCK, seq)
    for p in range(len(B_PATTERNS)):
        kp[p, 0:B_BLOCK, :] = jnp.zeros((B_BLOCK, B_HEAD_DIM), BF16)
        vp[p, 0:B_BLOCK, :] = jnp.zeros((B_BLOCK, 2 * B_HEAD_DIM), BF16)
        vp[p, data, B_HEAD_DIM:] = jnp.ones((seq, B_HEAD_DIM), BF16)
    qp[0] = qn.astype(BF16)
    kp[0, data, :] = kn.astype(BF16)
    vp[0, data, 0:B_HEAD_DIM] = v.astype(BF16)
    os_[0] = qn
    for p, (win, dil) in enumerate(B_PATTERNS):
        if dil == 1:
            continue
        ln = seq // dil
        for r in range(dil):
            src = pl.ds(r, ln, stride=dil)
            qp[p, pl.ds(r * ln, ln), :] = os_[0, src, :].astype(BF16)
            kp[p, pl.ds(B_BLOCK + r * ln, ln), :] = ko_ref[src, :].astype(BF16)
            vp[p, pl.ds(B_BLOCK + r * ln, ln), 0:B_HEAD_DIM] = v_ref[src, :].astype(BF16)

    qi = lax.broadcasted_iota(jnp.int32, (B_BLOCK, B_BLOCK), 0)
    kj = lax.broadcasted_iota(jnp.int32, (B_BLOCK, B_BLOCK), 1)
    mask_cur = kj <= qi
    mask_prev = kj >= qi

    def group(p, dil, g0, firsts, position0):
        n_q = len(firsts) * B_BLOCK
        base = g0 * B_BLOCK
        if not isinstance(base, int):
            base = pl.multiple_of(base, B_BLOCK)
        s = _dot_nt(qp[p, pl.ds(base, n_q), :], kp[p, pl.ds(base, n_q + B_BLOCK), :])
        for u, first in enumerate(firsts):
            lo, mid, hi = u * B_BLOCK, (u + 1) * B_BLOCK, (u + 2) * B_BLOCK
            s_cur = jnp.where(mask_cur, s[lo:mid, mid:hi], NEG_INF)
            if first:
                m = jnp.max(s_cur, axis=-1, keepdims=True)
                res = _dot(jnp.exp(s_cur - m).astype(BF16), vp[p, pl.ds(base + mid, B_BLOCK), :])
            else:
                s_prev = jnp.where(mask_prev, s[lo:mid, lo:mid], NEG_INF)
                m = jnp.max(jnp.maximum(s_cur, s_prev), axis=-1, keepdims=True)
                probs = jnp.concatenate([jnp.exp(s_prev - m).astype(BF16), jnp.exp(s_cur - m).astype(BF16)], axis=1)
                res = _dot(probs, vp[p, pl.ds(base + lo, 2 * B_BLOCK), :])
            den = res[:, B_HEAD_DIM:]
            rows = pl.ds(position0(u), B_BLOCK, stride=dil) if dil > 1 else pl.ds(base + lo, B_BLOCK)
            os_[p, rows, :] = res[:, :B_HEAD_DIM] * (1.0 / den)
            ls[p, rows, :] = m + jnp.log(den)

    for p, (win, dil) in enumerate(B_PATTERNS):
        nb = seq // (dil * B_BLOCK)
        if nb == 1:
            def body(t, carry, p=p, dil=dil):
                group(p, dil, t * ATTN_GROUP, (True,) * ATTN_GROUP, lambda u: t * ATTN_GROUP + u)
                return carry

            lax.fori_loop(0, dil // ATTN_GROUP, body, 0, unroll=True)
            continue
        head = (True,) + (False,) * (ATTN_GROUP - 1)
        tail = (False,) * ATTN_GROUP
        per_class = nb // ATTN_GROUP

        def one_class(r, carry, p=p, dil=dil, nb=nb, per_class=per_class, head=head, tail=tail):
            group(p, dil, r * nb, head, lambda u: u * (B_BLOCK * dil) + r)
            if per_class > 1:
                def rest(t, c2):
                    group(p, dil, r * nb + t * ATTN_GROUP, tail,
                          lambda u: (t * ATTN_GROUP + u) * (B_BLOCK * dil) + r)
                    return c2

                lax.fori_loop(1, per_class, rest, 0, unroll=True)
            return carry

        if dil == 1:
            one_class(0, 0)
        else:
            lax.fori_loop(0, dil, one_class, 0, unroll=True)

    l0, l1, l2 = ls[0], ls[1], ls[2]
    mx = jnp.maximum(jnp.maximum(l0, l1), l2)
    e0, e1, e2 = jnp.exp(l0 - mx), jnp.exp(l1 - mx), jnp.exp(l2 - mx)
    out = (e0 * os_[0] + e1 * os_[1] + e2 * os_[2]) * (1.0 / (e0 + e1 + e2))
    o_ref[...] = out.astype(o_ref.dtype)


def _attn_prompt(proj, qg, kg, *, n_seq, seq, col0, out_rows, layer, depth, kv_prev=None):
    assert B_PATTERNS[0][1] == 1 and len(B_PATTERNS) == 3
    for _, dil in B_PATTERNS:
        nb = seq // (dil * B_BLOCK)
        assert seq % (dil * B_BLOCK) == 0 and (nb % ATTN_GROUP == 0 or (nb == 1 and dil % ATTN_GROUP == 0))
    width = B_HEADS * B_HEAD_DIM
    blk = (seq, B_HEAD_DIM)

    def in_spec(part):
        return pl.BlockSpec(blk, lambda i, h: (i, col0 + part * B_HEADS + h))

    g_spec = pl.BlockSpec((1, B_HEAD_DIM), lambda i, h: (0, 0))
    kv_spec = pl.BlockSpec((None, None, seq, B_HEAD_DIM), lambda i, h: (layer, i, 0, h))
    kv_shape = jax.ShapeDtypeStruct((depth, n_seq, seq, width), F32)
    in_specs = [in_spec(0), in_spec(1), in_spec(2), g_spec, g_spec]
    args = [proj, proj, proj, qg.reshape(1, -1), kg.reshape(1, -1)]
    aliases = {}
    if kv_prev is not None:
        in_specs += [pl.BlockSpec(memory_space=pl.ANY)] * 2
        args += list(kv_prev)
        aliases = {5: 1, 6: 2}
    n_pat = len(B_PATTERNS)
    return pl.pallas_call(
        functools.partial(_attn_prompt_kernel, seq=seq, aliased=kv_prev is not None),
        grid=(n_seq, B_HEADS),
        in_specs=in_specs,
        out_specs=[pl.BlockSpec(blk, lambda i, h: (i, h)), kv_spec, kv_spec],
        out_shape=[jax.ShapeDtypeStruct((out_rows, width), BF16), kv_shape, kv_shape],
        input_output_aliases=aliases,
        scratch_shapes=[pltpu.VMEM((n_pat, seq, B_HEAD_DIM), BF16),
                        pltpu.VMEM((n_pat, seq + B_BLOCK, B_HEAD_DIM), BF16),
                        pltpu.VMEM((n_pat, seq + B_BLOCK, 2 * B_HEAD_DIM), BF16)]
        + [pltpu.VMEM((n_pat, seq, B_HEAD_DIM), F32)] * 2,
        compiler_params=_params(("parallel", "parallel")),
    )(*args)


SAMPLE_HEADS_PER_STEP = 8


def _divmod_const(x, d):
    if d & (d - 1) == 0:
        return x >> (d.bit_length() - 1), x & (d - 1)
    return x // d, x % d


def _attn_sample_kernel(q_ref, k_ref, v_ref, kc_ref, vc_ref, qg_ref, kg_ref, o_ref, ko_ref, vo_ref,
                        qs, ks, vs, *, t_new, buf):
    hs = SAMPLE_HEADS_PER_STEP
    n_rows = hs * t_new
    for hh in range(hs):
        cols = slice(hh * B_HEAD_DIM, (hh + 1) * B_HEAD_DIM)
        rows = slice(hh * t_new, (hh + 1) * t_new)
        kn = _qk_norm(k_ref[0, :, cols], kg_ref[...], 1.0)
        ko_ref[0, :, cols] = kn
        vo_ref[0, :, cols] = v_ref[0, :, cols]
        qs[rows, :] = _qk_norm(q_ref[0, :, cols], qg_ref[...], B_HEAD_DIM ** -0.5)
        ks[rows, :] = kn
        vs[rows, :] = v_ref[0, :, cols]

    qb = qs[...].astype(BF16)
    s_c = _dot_nt(qb, kc_ref[...].reshape(buf * hs, B_HEAD_DIM).astype(BF16))
    s_n = _dot_nt(qb, ks[...].astype(BF16))
    vcb = vc_ref[...].reshape(buf * hs, B_HEAD_DIM).astype(BF16)
    vnew = vs[...].astype(BF16)

    row_c = lax.broadcasted_iota(jnp.int32, (n_rows, buf * hs), 0)
    col_c = lax.broadcasted_iota(jnp.int32, (n_rows, buf * hs), 1)
    head_r, tok_r = _divmod_const(row_c, t_new)
    cache_row, head_c = _divmod_const(col_c, hs)
    same_c = head_r == head_c
    dist_c = buf + tok_r - cache_row
    row_n = lax.broadcasted_iota(jnp.int32, (n_rows, n_rows), 0)
    col_n = lax.broadcasted_iota(jnp.int32, (n_rows, n_rows), 1)
    head_rn, tok_rn = _divmod_const(row_n, t_new)
    head_cn, tok_cn = _divmod_const(col_n, t_new)
    dist_n = tok_rn - tok_cn
    same_n = jnp.logical_and(head_rn == head_cn, dist_n >= 0)

    outs, lses = [], []
    for win, dil in B_PATTERNS:
        ok_c = jnp.logical_and(same_c, jnp.logical_and((dist_c & (dil - 1)) == 0, dist_c <= win))
        ok_n = jnp.logical_and(same_n, (dist_n & (dil - 1)) == 0)
        sc = jnp.where(ok_c, s_c, NEG_INF)
        sn = jnp.where(ok_n, s_n, NEG_INF)
        m = jnp.maximum(jnp.max(sc, axis=-1, keepdims=True), jnp.max(sn, axis=-1, keepdims=True))
        pc = jnp.exp(sc - m)
        pn = jnp.exp(sn - m)
        den = jnp.sum(pc, axis=-1, keepdims=True) + jnp.sum(pn, axis=-1, keepdims=True)
        num = _dot(pc.astype(BF16), vcb) + _dot(pn.astype(BF16), vnew)
        outs.append(num * (1.0 / den))
        lses.append(m + jnp.log(den))
    mx = jnp.maximum(jnp.maximum(lses[0], lses[1]), lses[2])
    es = [jnp.exp(l - mx) for l in lses]
    out = (es[0] * outs[0] + es[1] * outs[1] + es[2] * outs[2]) * (1.0 / (es[0] + es[1] + es[2]))
    for hh in range(hs):
        o_ref[0, :, hh * B_HEAD_DIM:(hh + 1) * B_HEAD_DIM] = out[hh * t_new:(hh + 1) * t_new, :]


def _attn_sample(qkv, cache_k, cache_v, qg, kg, *, layer):
    n, t_new, _ = qkv.shape
    buf = cache_k.shape[2]
    assert cache_k.shape[3:] == (B_HEADS, B_HEAD_DIM)
    assert all(win <= buf for win, _ in B_PATTERNS) and all(dil & (dil - 1) == 0 for _, dil in B_PATTERNS)
    assert (SAMPLE_HEADS_PER_STEP * t_new) % 8 == 0
    width = B_HEADS * B_HEAD_DIM
    hw = SAMPLE_HEADS_PER_STEP * B_HEAD_DIM
    steps = width // hw
    blk = (1, t_new, hw)

    def in_spec(part):
        return pl.BlockSpec(blk, lambda i, h: (i, 0, part * steps + h))

    c_spec = pl.BlockSpec((None, None, buf, SAMPLE_HEADS_PER_STEP, B_HEAD_DIM), lambda i, h: (layer, i, 0, h, 0))
    g_spec = pl.BlockSpec((1, B_HEAD_DIM), lambda i, h: (0, 0))
    out_spec = pl.BlockSpec(blk, lambda i, h: (i, 0, h))
    return pl.pallas_call(
        functools.partial(_attn_sample_kernel, t_new=t_new, buf=buf),
        grid=(n, steps),
        in_specs=[in_spec(0), in_spec(1), in_spec(2), c_spec, c_spec, g_spec, g_spec],
        out_specs=[out_spec, out_spec, out_spec],
        out_shape=[jax.ShapeDtypeStruct((n, t_new, width), F32)] * 3,
        scratch_shapes=[pltpu.VMEM((SAMPLE_HEADS_PER_STEP * t_new, B_HEAD_DIM), F32)] * 3,
        compiler_params=_params(("parallel", "parallel")),
    )(qkv, qkv, qkv, cache_k, cache_v, qg.reshape(1, -1), kg.reshape(1, -1))


def _split3(x):
    hi = x.astype(BF16)
    r1 = x - hi.astype(F32)
    mid = r1.astype(BF16)
    lo = (r1 - mid.astype(F32)).astype(BF16)
    return hi, mid, lo


def _mlstm_kernel(q_ref, k_ref, v_ref, o_ref, ifc_ref, ifr_ref, bc_ref, br_ref, hg_ref, c0_ref, n0_ref, m0_ref,
                  h_ref, cf_ref, nf_ref, mf_ref, ct_s, n_s, m_s, *, ln):
    j = pl.program_id(1)
    last = pl.num_programs(1) - 1

    @pl.when(j == 0)
    def _():
        for h in range(C_HEADS):
            ct_s[h] = c0_ref[0, h].T
        n_s[...] = n0_ref[0]
        m_s[...] = m0_ref[0]

    ifc = ifc_ref[...] + bc_ref[...]
    ifr = ifr_ref[0] + br_ref[...]
    row = lax.broadcasted_iota(jnp.int32, (ln, ln), 0)
    col = lax.broadcasted_iota(jnp.int32, (ln, ln), 1)
    causal = col <= row
    tri = jnp.where(causal, 1.0, 0.0).astype(BF16)
    triu = jnp.where(row <= col, 1.0, 0.0).astype(BF16)
    b_col = sum(_dot(tri, part) for part in _split3(_log_sigmoid(ifc)))
    b_row = sum(_dot(part, triu) for part in _split3(_log_sigmoid(ifr)))

    for h in range(C_HEADS):
        qc = q_ref[:, h * C_QK_DIM:(h + 1) * C_QK_DIM]
        qb = qc.astype(BF16)
        kc = k_ref[:, h * C_QK_DIM:(h + 1) * C_QK_DIM] * (C_QK_DIM ** -0.5)
        kb = kc.astype(BF16)
        vc = v_ref[:, h * C_V_DIM:(h + 1) * C_V_DIM]
        bcol = b_col[:, C_HEADS + h:C_HEADS + h + 1]
        brow = b_row[C_HEADS + h:C_HEADS + h + 1, :]
        icol = ifc[:, h:h + 1]
        irow = ifr[h:h + 1, :]
        m_prev = m_s[h:h + 1, 0:1]
        n_prev = n_s[h:h + 1, :]
        ct_prev = ct_s[h]

        dmat = jnp.where(causal, bcol - brow + irow, NEG_INF)
        inter = bcol + m_prev
        mt = jnp.maximum(inter, jnp.max(dmat, axis=-1, keepdims=True))
        wmat = jnp.exp(dmat - mt) * _dot_nt(qb, kb)
        w_inter = jnp.exp(inter - mt)
        num = _dot(wmat.astype(BF16), vc.astype(BF16)) + w_inter * _dot(qb, ct_prev.astype(BF16))
        den = jnp.sum(wmat, axis=-1, keepdims=True) + w_inter * jnp.sum(qc * n_prev, axis=-1, keepdims=True)
        hc = num * (1.0 / jnp.maximum(jnp.abs(den), jnp.exp(-mt)))

        m_new = mt[ln - 1:ln, :]
        b_last = bcol[ln - 1:ln, :]
        w_end = jnp.exp(b_last - bcol + icol - m_new)
        decay = jnp.exp(b_last + m_prev - m_new)
        ct_s[h] = decay * ct_prev + _dot(kc.T.astype(BF16), (w_end * vc).astype(BF16))
        n_s[h:h + 1, :] = decay * n_prev + jnp.sum(w_end * kc, axis=0, keepdims=True)
        m_s[h:h + 1, :] = jnp.broadcast_to(m_new, (1, LANES))

        gain = hg_ref[:, h * C_V_DIM:(h + 1) * C_V_DIM]
        hn = hc * lax.rsqrt(jnp.mean(hc * hc, axis=-1, keepdims=True) + EPS) * gain
        og = _sigmoid(o_ref[:, h * C_V_DIM:(h + 1) * C_V_DIM])
        h_ref[:, h * C_V_DIM:(h + 1) * C_V_DIM] = (og * hn).astype(h_ref.dtype)

    @pl.when(j == last)
    def _():
        for h in range(C_HEADS):
            cf_ref[0, h] = ct_s[h].T
        nf_ref[0] = n_s[...]
        mf_ref[0] = m_s[...]


def _mlstm(proj, ifc, ifr, bias_c, bias_r, hn_g, c0, n0, m0, *, n_seq, seq, col_q, col_k, col_v, col_o, ln,
           out_rows):
    ln = min(ln, seq)
    nc = seq // ln
    assert seq % ln == 0 and ln % LANES == 0
    qk_w = C_HEADS * C_QK_DIM
    v_w = C_HEADS * C_V_DIM
    assert col_q % qk_w == 0 and col_k % qk_w == 0 and col_v % v_w == 0 and col_o % v_w == 0
    state_specs = [pl.BlockSpec((1, C_HEADS, C_V_DIM, C_QK_DIM), lambda i, j: (i, 0, 0, 0)),
                   pl.BlockSpec((1, C_HEADS, C_QK_DIM), lambda i, j: (i, 0, 0)),
                   pl.BlockSpec((1, C_HEADS, LANES), lambda i, j: (i, 0, 0))]
    return pl.pallas_call(
        functools.partial(_mlstm_kernel, ln=ln),
        grid=(n_seq, nc),
        in_specs=[pl.BlockSpec((ln, qk_w), lambda i, j: (i * nc + j, col_q // qk_w)),
                  pl.BlockSpec((ln, qk_w), lambda i, j: (i * nc + j, col_k // qk_w)),
                  pl.BlockSpec((ln, v_w), lambda i, j: (i * nc + j, col_v // v_w)),
                  pl.BlockSpec((ln, v_w), lambda i, j: (i * nc + j, col_o // v_w)),
                  pl.BlockSpec((ln, LANES), lambda i, j: (i * nc + j, 0)),
                  pl.BlockSpec((1, 2 * C_HEADS, ln), lambda i, j: (i, 0, j)),
                  pl.BlockSpec((1, LANES), lambda i, j: (0, 0)),
                  pl.BlockSpec((2 * C_HEADS, 1), lambda i, j: (0, 0)),
                  pl.BlockSpec((1, v_w), lambda i, j: (0, 0))] + state_specs,
        out_specs=[pl.BlockSpec((ln, v_w), lambda i, j: (i * nc + j, 0))] + state_specs,
        out_shape=[jax.ShapeDtypeStruct((out_rows, v_w), BF16),
                   jax.ShapeDtypeStruct((n_seq, C_HEADS, C_V_DIM, C_QK_DIM), F32),
                   jax.ShapeDtypeStruct((n_seq, C_HEADS, C_QK_DIM), F32),
                   jax.ShapeDtypeStruct((n_seq, C_HEADS, LANES), F32)],
        scratch_shapes=[pltpu.VMEM((C_HEADS, C_QK_DIM, C_V_DIM), F32),
                        pltpu.VMEM((C_HEADS, C_QK_DIM), F32),
                        pltpu.VMEM((C_HEADS, LANES), F32)],
        compiler_params=_params(("parallel", "arbitrary")),
    )(proj, proj, proj, proj, ifc, ifr, bias_c, bias_r, hn_g.reshape(1, v_w), c0, n0, m0)


MLSTM_CHUNK = 256
GATE_PAD = 1e30
ROW_TILES = 4
MERGE_ROW_TILES = 6
FF2_SPLITS = 2


def _put_rows(full, rows, row0, n_pad):
    block = jnp.pad(rows.astype(full.dtype), ((0, n_pad - rows.shape[0]), (0, 0)))
    return lax.dynamic_update_slice(full, block, (row0, 0))


def kernel(x_prompt, x_sample, cache_swa_k, cache_swa_v, state_mlstm_C, state_mlstm_n, state_mlstm_m, norm1_g, w_in, ws_a, bs_a, norm_va_g, qn_g, kn_g, i_b, f_b, hn_c_g, w_branch_a, w_branch_b, w_branch_c, w_out, norm2_g, w_ff1, w_ff2):
    depth = w_in.shape[0]
    nbp, seq, d = x_prompt.shape
    nbs, t_new, _ = x_sample.shape
    buf = cache_swa_k.shape[2]
    hd = B_HEADS * B_HEAD_DIM
    a_width = d // 2
    qk_w = C_HEADS * C_QK_DIM
    v_w = C_HEADS * C_V_DIM
    n_gate = N_BRANCH * d
    n_main = w_in.shape[2] - n_gate - IF_COLS
    col_b = 2 * a_width
    col_c = col_b + 3 * hd
    assert col_c + 2 * qk_w + 2 * v_w == n_main and seq % A_CHUNK == 0 and t_new <= A_CHUNK

    mp, ms = nbp * seq, nbs * t_new
    quantum = BF16_SUBLANES * math.lcm(ROW_TILES, MERGE_ROW_TILES)
    m_tot = -(-(mp + ms) // quantum) * quantum
    n_tail = m_tot - mp
    tm = m_tot // ROW_TILES
    tm_merge = m_tot // MERGE_ROW_TILES
    tm_norm = m_tot // (ROW_TILES * 3)
    assert mp % n_tail == 0 and n_tail % t_new == 0 and tm_norm % BF16_SUBLANES == 0
    x = jnp.concatenate([x_prompt.reshape(mp, d), x_sample.reshape(ms, d), jnp.zeros((m_tot - mp - ms, d), F32)])

    w_in_t = jnp.swapaxes(w_in, 1, 2)
    zero_state = (jnp.zeros((nbp, C_HEADS, C_V_DIM, C_QK_DIM), F32), jnp.zeros((nbp, C_HEADS, C_QK_DIM), F32),
                  jnp.zeros((nbp, C_HEADS, LANES), F32))
    lane = jnp.arange(LANES)
    gate_pad_row = jnp.where(lane < C_HEADS, -GATE_PAD, jnp.where(lane < IF_COLS, GATE_PAD, 0.0)).astype(F32)
    seqs_tail = n_tail // t_new

    kv_prompt = None
    small = [[] for _ in range(9)]
    for l in range(depth):
        w_if = jnp.pad(w_in_t[l, n_main:n_main + IF_COLS, :], ((0, LANES - IF_COLS), (0, 0)))
        bias = jnp.concatenate([i_b[l], f_b[l]])
        bias_c = jnp.pad(bias, (0, LANES - IF_COLS)).reshape(1, LANES)
        bias_r = bias.reshape(IF_COLS, 1)

        xn = _rmsnorm_bf16(x, norm1_g[l], tm=tm_norm)
        proj = _matmul(xn, w_in_t, layer=l, n_cols=n_main, tm=tm, w_transposed=True)
        ifp = _matmul(xn, w_if, tm=tm, w_transposed=True)
        gates = _matmul(xn, w_in_t, layer=l, n_cols=n_gate, col_off=n_main, row_shift=IF_COLS, tm=tm,
                        w_transposed=True)

        a_out, _ = _gmlp(proj, ws_a[l], bs_a[l].T, norm_va_g[l], width=a_width, chunk=A_CHUNK, chunks=2,
                         row0=0, n_rows=mp, out_rows=m_tot, want_va=False)
        eye = jnp.eye(seqs_tail, dtype=F32)
        ws_tail = jax.vmap(lambda w: jnp.kron(eye, w[:t_new, :t_new]))(ws_a[l])
        bs_tail = jnp.tile(bs_a[l][:, :t_new], (1, seqs_tail)).T
        a_tail, va_tail = _gmlp(proj, ws_tail, bs_tail, norm_va_g[l], width=a_width, chunk=n_tail, chunks=1,
                                row0=mp, n_rows=n_tail, out_rows=n_tail, want_va=True)
        a_out = lax.dynamic_update_slice(a_out, a_tail, (mp, 0))

        b_out, kp, vp = _attn_prompt(proj, qn_g[l], kn_g[l], n_seq=nbp, seq=seq, col0=col_b // B_HEAD_DIM,
                                     out_rows=m_tot, layer=l, depth=depth, kv_prev=kv_prompt)
        kv_prompt = (kp, vp)
        qkv_s = proj[mp:mp + ms, col_b:col_c].reshape(nbs, t_new, 3 * hd)
        b_s, k_s, v_s = _attn_sample(qkv_s, cache_swa_k, cache_swa_v, qn_g[l], kn_g[l], layer=l)
        b_out = _put_rows(b_out, b_s.reshape(ms, hd), mp, n_tail)

        cols = dict(col_q=col_c, col_k=col_c + qk_w, col_v=col_c + 2 * qk_w, col_o=col_c + 2 * qk_w + v_w)
        ifr = jnp.swapaxes(ifp[:mp, :IF_COLS].reshape(nbp, seq, IF_COLS), 1, 2)
        c_out, cp, n_p, m_p = _mlstm(proj, ifp, ifr, bias_c, bias_r, hn_c_g[l], *zero_state, n_seq=nbp, seq=seq,
                                     ln=MLSTM_CHUNK, out_rows=m_tot, **cols)
        pad = A_CHUNK - t_new
        proj_s = jnp.pad(proj[mp:mp + ms, col_c:].reshape(nbs, t_new, n_main - col_c), ((0, 0), (0, pad), (0, 0)))
        ifc_s = jnp.concatenate([ifp[mp:mp + ms].reshape(nbs, t_new, LANES),
                                 jnp.broadcast_to(gate_pad_row, (nbs, pad, LANES))], axis=1)
        ifr_s = jnp.swapaxes(ifc_s[:, :, :IF_COLS], 1, 2)
        m0 = jnp.broadcast_to(state_mlstm_m[l][..., None], (nbs, C_HEADS, LANES))
        c_s_out, c_s, n_s, m_s = _mlstm(
            proj_s.reshape(nbs * A_CHUNK, -1), ifc_s.reshape(nbs * A_CHUNK, LANES), ifr_s, bias_c, bias_r, hn_c_g[l],
            state_mlstm_C[l], state_mlstm_n[l], m0, n_seq=nbs, seq=A_CHUNK, ln=A_CHUNK, out_rows=nbs * A_CHUNK,
            col_q=0, col_k=qk_w, col_v=2 * qk_w, col_o=2 * qk_w + v_w)
        c_out = _put_rows(c_out, c_s_out.reshape(nbs, A_CHUNK, v_w)[:, :t_new].reshape(ms, v_w), mp, n_tail)

        merged = _merge(a_out, b_out, c_out, w_branch_a, w_branch_b, w_branch_c, gates, layer=l, tm=tm)
        x = _matmul(merged, w_out, layer=l, epilogue="resid", resid=x, tm=tm, tn=256)
        h = _matmul(_rmsnorm_bf16(x, norm2_g[l], tm=tm_norm), w_ff1, layer=l, epilogue="relu2", out_dtype=BF16, tm=tm)
        k_len = w_ff2.shape[1] // FF2_SPLITS
        for part in range(FF2_SPLITS):
            final = l == depth - 1 and part == FF2_SPLITS - 1
            x = _matmul(h, w_ff2, layer=l, k_off=part * k_len, k_len=k_len, epilogue="resid", resid=x,
                        tm=tm_merge, tn=256, main_rows=mp if final else None)

        vals = (cp, n_p, m_p[..., 0], k_s.reshape(nbs, t_new, B_HEADS, B_HEAD_DIM),
                v_s.reshape(nbs, t_new, B_HEADS, B_HEAD_DIM), c_s, n_s, m_s[..., 0],
                va_tail[:ms].reshape(nbs, t_new, a_width))
        for o, v in zip(small, vals):
            o.append(v)

    keep = min(max(w for w, _ in B_PATTERNS), seq)
    kp, vp = (t[:, :, seq - keep:].reshape(depth, nbp, keep, B_HEADS, B_HEAD_DIM) for t in kv_prompt)
    st = [jnp.stack(o) for o in small]
    y_prompt, y_tail = x
    return (y_prompt.reshape(nbp, seq, d), y_tail[:ms].reshape(nbs, t_new, d), kp, vp, st[0], st[1], st[2],
            st[3], st[4], st[5], st[6], st[7], st[8])
```

```python
import functools
import math

import jax
import jax.numpy as jnp
from jax import lax
from jax.experimental import pallas as pl
from jax.experimental.pallas import tpu as pltpu

F32 = jnp.float32
BF16 = jnp.bfloat16

EPS = 1e-6
NEG_INF = -1e30

A_CHUNK = 128
A_GROUPS = 8
B_HEADS = 16
B_HEAD_DIM = 128
B_PATTERNS = ((128, 1), (512, 4), (2048, 16))
B_BLOCK = 128
C_HEADS = 8
C_QK_DIM = 128
C_V_DIM = 256
N_BRANCH = 3
IF_COLS = 2 * C_HEADS

V7X_VMEM_BYTES = 64 * 1024 * 1024
VMEM_LIMIT_BYTES = V7X_VMEM_BYTES - 8 * 1024 * 1024
WIDE_VMEM_LIMIT_BYTES = V7X_VMEM_BYTES - 3 * 1024 * 1024
LANES = 128
BF16_SUBLANES = 16
SINGLE_BUFFER_BYTES = 4 << 20


def _params(semantics, vmem_limit_bytes=VMEM_LIMIT_BYTES):
    return pltpu.CompilerParams(dimension_semantics=semantics, vmem_limit_bytes=vmem_limit_bytes)


def _gelu(x):
    c = math.sqrt(2.0 / math.pi)
    return 0.5 * x * (1.0 + jnp.tanh(c * (x + 0.044715 * (x * x * x))))


def _sigmoid(x):
    return 1.0 / (1.0 + jnp.exp(-x))


def _log_sigmoid(x):
    return -(jnp.maximum(-x, 0.0) + jnp.log(1.0 + jnp.exp(-jnp.abs(x))))


def _dot(a, b):
    return jnp.dot(a, b, preferred_element_type=F32)


def _dot_nt(a, b):
    return lax.dot_general(a, b, (((1,), (1,)), ((), ())), preferred_element_type=F32)


def _rmsnorm_kernel(x_ref, g_ref, o_ref):
    x = x_ref[...]
    y = x * lax.rsqrt(jnp.mean(x * x, axis=-1, keepdims=True) + EPS)
    o_ref[...] = (y * g_ref[...]).astype(o_ref.dtype)


def _rmsnorm_bf16(x, g, *, tm):
    m, d = x.shape
    assert m % tm == 0
    return pl.pallas_call(
        _rmsnorm_kernel,
        grid=(m // tm,),
        in_specs=[pl.BlockSpec((tm, d), lambda i: (i, 0)), pl.BlockSpec((1, d), lambda i: (0, 0))],
        out_specs=pl.BlockSpec((tm, d), lambda i: (i, 0)),
        out_shape=jax.ShapeDtypeStruct((m, d), BF16),
        compiler_params=_params(("parallel",)),
    )(x, g.reshape(1, d))


def _mm_kernel(x_ref, w_ref, *rest, epilogue, w_transposed, tail):
    n_out = 2 if tail else 1
    outs = rest[-n_out:]
    extra = rest[:-n_out]
    w = (w_ref[0] if len(w_ref.shape) == 3 else w_ref[...]).astype(BF16)
    acc = _dot_nt(x_ref[...], w) if w_transposed else _dot(x_ref[...], w)
    if epilogue == "relu2":
        r = jnp.maximum(acc, 0.0)
        acc = r * r
    elif epilogue == "resid":
        acc = acc + extra[-1][...]
    outs[0][...] = acc.astype(outs[0].dtype)
    if tail:
        lo, n_tail = tail
        outs[1][...] = acc[lo:lo + n_tail, :].astype(outs[1].dtype)


def _matmul(x, w, *, tm, tn=512, layer=None, n_cols=None, col_off=0, k_off=0, k_len=None, epilogue="none",
            resid=None, out_dtype=F32, w_transposed=False, row_shift=0, main_rows=None,
            vmem_limit_bytes=VMEM_LIMIT_BYTES):
    m, kdim = x.shape
    k_len = kdim if k_len is None else k_len
    n = w.shape[-2 if w_transposed else -1] if n_cols is None else n_cols
    tn = min(tn, n)
    assert m % tm == 0 and n % tn == 0 and k_off % k_len == 0 and col_off % tn == 0
    assert row_shift == 0 or (w_transposed and row_shift % 8 == 0)
    joff, kblk = col_off // tn, k_off // k_len
    x_mode = pl.Buffered(1) if (n // tn > 1 and tm * k_len * 2 > SINGLE_BUFFER_BYTES) else None

    if row_shift:
        assert layer is not None
        w_spec = pl.BlockSpec((pl.Element(1), pl.Element(tn), pl.Element(k_len)),
                              lambda i, j: (layer, ((col_off + row_shift) // 8 + j * (tn // 8)) * 8, k_off))
    else:
        if w_transposed:
            w_blk, w_idx = (tn, k_len), lambda i, j: (j + joff, kblk)
        else:
            w_blk, w_idx = (k_len, tn), lambda i, j: (kblk, j + joff)
        if layer is None:
            w_spec = pl.BlockSpec(w_blk, w_idx)
        else:
            w_spec = pl.BlockSpec((None,) + w_blk, lambda i, j: (layer,) + w_idx(i, j))
    in_specs = [pl.BlockSpec((tm, k_len), lambda i, j: (i, kblk), pipeline_mode=x_mode), w_spec]
    args = [x, w]
    if epilogue == "resid":
        in_specs.append(pl.BlockSpec((tm, tn), lambda i, j: (i, j)))
        args.append(resid)
    out_specs = pl.BlockSpec((tm, tn), lambda i, j: (i, j))
    out_shape = jax.ShapeDtypeStruct((m, n), out_dtype)
    tail = None
    if main_rows is not None:
        n_tail = m - main_rows
        lo = main_rows - (m // tm - 1) * tm
        assert 0 <= lo and lo % 8 == 0 and lo + n_tail == tm
        tail = (lo, n_tail)
        out_specs = [out_specs, pl.BlockSpec((n_tail, tn), lambda i, j: (i, j))]
        out_shape = [jax.ShapeDtypeStruct((main_rows, n), out_dtype),
                     jax.ShapeDtypeStruct((m // tm * n_tail, n), out_dtype)]
    res = pl.pallas_call(
        functools.partial(_mm_kernel, epilogue=epilogue, w_transposed=w_transposed, tail=tail),
        grid=(m // tm, n // tn),
        in_specs=in_specs,
        out_specs=out_specs,
        out_shape=out_shape,
        compiler_params=_params(("parallel", "parallel"), vmem_limit_bytes),
    )(*args)
    if tail:
        return res[0], res[1][(m // tm - 1) * tail[1]:]
    return res


def _merge_kernel(a_ref, b_ref, c_ref, wa_ref, wb_ref, wc_ref, ga_ref, gb_ref, gc_ref, o_ref):
    acc = _sigmoid(ga_ref[...]) * _dot(a_ref[...], wa_ref[...].astype(BF16))
    acc += _sigmoid(gb_ref[...]) * _dot(b_ref[...], wb_ref[...].astype(BF16))
    acc += _sigmoid(gc_ref[...]) * _dot(c_ref[...], wc_ref[...].astype(BF16))
    o_ref[...] = acc.astype(o_ref.dtype)


def _merge(a, b, c, wa, wb, wc, gates, *, layer, tm, tn=256):
    m, kdim = a.shape
    n = wa.shape[-1]
    assert m % tm == 0 and n % tn == 0 and gates.shape == (m, N_BRANCH * n)
    nj = n // tn
    x_mode = pl.Buffered(1) if tm * kdim * 2 > SINGLE_BUFFER_BYTES else None

    def x_spec():
        return pl.BlockSpec((tm, kdim), lambda i, j: (i, 0), pipeline_mode=x_mode)

    def g_spec(branch):
        return pl.BlockSpec((tm, tn), lambda i, j: (i, j + branch * nj))

    w_spec = pl.BlockSpec((None, kdim, tn), lambda i, j: (layer, 0, j))
    return pl.pallas_call(
        _merge_kernel,
        grid=(m // tm, nj),
        in_specs=[x_spec(), x_spec(), x_spec(), w_spec, w_spec, w_spec, g_spec(0), g_spec(1), g_spec(2)],
        out_specs=pl.BlockSpec((tm, tn), lambda i, j: (i, j)),
        out_shape=jax.ShapeDtypeStruct((m, n), BF16),
        compiler_params=_params(("parallel", "parallel")),
    )(a, b, c, wa, wb, wc, gates, gates, gates)


def _gmlp_kernel(u_ref, v_ref, w_ref, b_ref, g_ref, a_ref, *va_refs, chunk, chunks, gdim):
    row = lax.broadcasted_iota(jnp.int32, (chunk, chunk), 0)
    col = lax.broadcasted_iota(jnp.int32, (chunk, chunk), 1)
    causal = col <= row
    w_groups = [jnp.where(causal, w_ref[g], 0.0).astype(BF16) for g in range(A_GROUPS)]
    gain = g_ref[...]
    bias = b_ref[...]
    for c in range(chunks):
        rows = pl.ds(c * chunk, chunk)
        v = _gelu(v_ref[rows, :])
        vn = v * lax.rsqrt(jnp.mean(v * v, axis=-1, keepdims=True) + EPS) * gain
        if va_refs:
            va_refs[0][rows, :] = vn
        vb = vn.astype(BF16)
        for g in range(A_GROUPS):
            cols = slice(g * gdim, (g + 1) * gdim)
            mixed = _dot(w_groups[g], vb[:, cols]) + bias[:, g:g + 1]
            u = _gelu(u_ref[rows, cols])
            a_ref[rows, cols] = (u * mixed).astype(a_ref.dtype)


def _gmlp(proj, ws, bs_t, norm_g, *, width, chunk, chunks, row0, n_rows, out_rows, want_va):
    rows = chunk * chunks
    assert n_rows % rows == 0 and row0 % rows == 0
    blk0 = row0 // rows
    gdim = width // A_GROUPS
    out_shape = [jax.ShapeDtypeStruct((out_rows, width), BF16)]
    out_specs = [pl.BlockSpec((rows, width), lambda j: (j, 0))]
    if want_va:
        out_shape.append(jax.ShapeDtypeStruct((out_rows, width), F32))
        out_specs.append(pl.BlockSpec((rows, width), lambda j: (j, 0)))
    res = pl.pallas_call(
        functools.partial(_gmlp_kernel, chunk=chunk, chunks=chunks, gdim=gdim),
        grid=(n_rows // rows,),
        in_specs=[pl.BlockSpec((rows, width), lambda j: (blk0 + j, 0)),
                  pl.BlockSpec((rows, width), lambda j: (blk0 + j, 1)),
                  pl.BlockSpec((A_GROUPS, chunk, chunk), lambda j: (0, 0, 0)),
                  pl.BlockSpec((chunk, A_GROUPS), lambda j: (0, 0)),
                  pl.BlockSpec((1, width), lambda j: (0, 0))],
        out_specs=out_specs,
        out_shape=out_shape,
        compiler_params=_params(("parallel",)),
    )(proj, proj, ws, bs_t, norm_g.reshape(1, width))
    return res if want_va else (res[0], None)


def _qk_norm(x, g, scale):
    return x * lax.rsqrt(jnp.mean(x * x, axis=-1, keepdims=True) + EPS) * (g * scale)


ATTN_GROUP = 4


def _attn_prompt_kernel(*refs, seq, aliased):
    q_ref, k_ref, v_ref, qg_ref, kg_ref = refs[:5]
    o_ref, ko_ref, vo_ref, qp, kp, vp, os_, ls = refs[5 + (2 if aliased else 0):]
    v = v_ref[...]
    kn = _qk_norm(k_ref[...], kg_ref[...], 1.0)
    qn = _qk_norm(q_ref[...], qg_ref[...], B_HEAD_DIM ** -0.5)
    ko_ref[...] = kn
    vo_ref[...] = v
    data = pl.ds(B_BLOCK, seq)
    for p in range(len(B_PATTERNS)):
        kp[p, 0:B_BLOCK, :] = jnp.zeros((B_BLOCK, B_HEAD_DIM), BF16)
        vp[p, 0:B_BLOCK, :] = jnp.zeros((B_BLOCK, 2 * B_HEAD_DIM), BF16)
        vp[p, data, B_HEAD_DIM:] = jnp.ones((seq, B_HEAD_DIM), BF16)
    qp[0] = qn.astype(BF16)
    kp[0, data, :] = kn.astype(BF16)
    vp[0, data, 0:B_HEAD_DIM] = v.astype(BF16)
    os_[0] = qn
    for p, (win, dil) in enumerate(B_PATTERNS):
        if dil == 1:
            continue
        ln = seq // dil
        for r in range(dil):
            src = pl.ds(r, ln, stride=dil)
            qp[p, pl.ds(r * ln, ln), :] = os_[0, src, :].astype(BF16)
            kp[p, pl.ds(B_BLOCK + r * ln, ln), :] = ko_ref[src, :].astype(BF16)
            vp[p, pl.ds(B_BLOCK + r * ln, ln), 0:B_HEAD_DIM] = v_ref[src, :].astype(BF16)

    qi = lax.broadcasted_iota(jnp.int32, (B_BLOCK, B_BLOCK), 0)
    kj = lax.broadcasted_iota(jnp.int32, (B_BLOCK, B_BLOCK), 1)
    mask_cur = kj <= qi
    mask_prev = kj >= qi

    def group(p, dil, g0, firsts, position0):
        n_q = len(firsts) * B_BLOCK
        base = g0 * B_BLOCK
        if not isinstance(base, int):
            base = pl.multiple_of(base, B_BLOCK)
        s = _dot_nt(qp[p, pl.ds(base, n_q), :], kp[p, pl.ds(base, n_q + B_BLOCK), :])
        for u, first in enumerate(firsts):
            lo, mid, hi = u * B_BLOCK, (u + 1) * B_BLOCK, (u + 2) * B_BLOCK
            s_cur = jnp.where(mask_cur, s[lo:mid, mid:hi], NEG_INF)
            if first:
                m = jnp.max(s_cur, axis=-1, keepdims=True)
                res = _dot(jnp.exp(s_cur - m).astype(BF16), vp[p, pl.ds(base + mid, B_BLOCK), :])
            else:
                s_prev = jnp.where(mask_prev, s[lo:mid, lo:mid], NEG_INF)
                m = jnp.max(jnp.maximum(s_cur, s_prev), axis=-1, keepdims=True)
                probs = jnp.concatenate([jnp.exp(s_prev - m).astype(BF16), jnp.exp(s_cur - m).astype(BF16)], axis=1)
                res = _dot(probs, vp[p, pl.ds(base + lo, 2 * B_BLOCK), :])
            den = res[:, B_HEAD_DIM:]
            rows = pl.ds(position0(u), B_BLOCK, stride=dil) if dil > 1 else pl.ds(base + lo, B_BLOCK)
            os_[p, rows, :] = res[:, :B_HEAD_DIM] * (1.0 / den)
            ls[p, rows, :] = m + jnp.log(den)

    for p, (win, dil) in enumerate(B_PATTERNS):
        nb = seq // (dil * B_BLOCK)
        if nb == 1:
            def body(t, carry, p=p, dil=dil):
                group(p, dil, t * ATTN_GROUP, (True,) * ATTN_GROUP, lambda u: t * ATTN_GROUP + u)
                return carry

            lax.fori_loop(0, dil // ATTN_GROUP, body, 0, unroll=True)
            continue
        head = (True,) + (False,) * (ATTN_GROUP - 1)
        tail = (False,) * ATTN_GROUP
        per_class = nb // ATTN_GROUP

        def one_class(r, carry, p=p, dil=dil, nb=nb, per_class=per_class, head=head, tail=tail):
            group(p, dil, r * nb, head, lambda u: u * (B_BLOCK * dil) + r)
            if per_class > 1:
                def rest(t, c2):
                    group(p, dil, r * nb + t * ATTN_GROUP, tail,
                          lambda u: (t * ATTN_GROUP + u) * (B_BLOCK * dil) + r)
                    return c2

                lax.fori_loop(1, per_class, rest, 0, unroll=True)
            return carry

        if dil == 1:
            one_class(0, 0)
        else:
            lax.fori_loop(0, dil, one_class, 0, unroll=True)

    l0, l1, l2 = ls[0], ls[1], ls[2]
    mx = jnp.maximum(jnp.maximum(l0, l1), l2)
    e0, e1, e2 = jnp.exp(l0 - mx), jnp.exp(l1 - mx), jnp.exp(l2 - mx)
    out = (e0 * os_[0] + e1 * os_[1] + e2 * os_[2]) * (1.0 / (e0 + e1 + e2))
    o_ref[...] = out.astype(o_ref.dtype)


def _attn_prompt(proj, qg, kg, *, n_seq, seq, col0, out_rows, layer, depth, kv_prev=None):
    assert B_PATTERNS[0][1] == 1 and len(B_PATTERNS) == 3
    for _, dil in B_PATTERNS:
        nb = seq // (dil * B_BLOCK)
        assert seq % (dil * B_BLOCK) == 0 and (nb % ATTN_GROUP == 0 or (nb == 1 and dil % ATTN_GROUP == 0))
    width = B_HEADS * B_HEAD_DIM
    blk = (seq, B_HEAD_DIM)

    def in_spec(part):
        return pl.BlockSpec(blk, lambda i, h: (i, col0 + part * B_HEADS + h))

    g_spec = pl.BlockSpec((1, B_HEAD_DIM), lambda i, h: (0, 0))
    kv_spec = pl.BlockSpec((None, None, seq, B_HEAD_DIM), lambda i, h: (layer, i, 0, h))
    kv_shape = jax.ShapeDtypeStruct((depth, n_seq, seq, width), F32)
    in_specs = [in_spec(0), in_spec(1), in_spec(2), g_spec, g_spec]
    args = [proj, proj, proj, qg.reshape(1, -1), kg.reshape(1, -1)]
    aliases = {}
    if kv_prev is not None:
        in_specs += [pl.BlockSpec(memory_space=pl.ANY)] * 2
        args += list(kv_prev)
        aliases = {5: 1, 6: 2}
    n_pat = len(B_PATTERNS)
    return pl.pallas_call(
        functools.partial(_attn_prompt_kernel, seq=seq, aliased=kv_prev is not None),
        grid=(n_seq, B_HEADS),
        in_specs=in_specs,
        out_specs=[pl.BlockSpec(blk, lambda i, h: (i, h)), kv_spec, kv_spec],
        out_shape=[jax.ShapeDtypeStruct((out_rows, width), BF16), kv_shape, kv_shape],
        input_output_aliases=aliases,
        scratch_shapes=[pltpu.VMEM((n_pat, seq, B_HEAD_DIM), BF16),
                        pltpu.VMEM((n_pat, seq + B_BLOCK, B_HEAD_DIM), BF16),
                        pltpu.VMEM((n_pat, seq + B_BLOCK, 2 * B_HEAD_DIM), BF16)]
        + [pltpu.VMEM((n_pat, seq, B_HEAD_DIM), F32)] * 2,
        compiler_params=_params(("parallel", "parallel")),
    )(*args)


SAMPLE_HEADS_PER_STEP = 8


def _divmod_const(x, d):
    if d & (d - 1) == 0:
        return x >> (d.bit_length() - 1), x & (d - 1)
    return x // d, x % d


def _attn_sample_kernel(q_ref, k_ref, v_ref, kc_ref, vc_ref, qg_ref, kg_ref, o_ref, ko_ref, vo_ref,
                        qs, ks, vs, *, t_new, buf):
    hs = SAMPLE_HEADS_PER_STEP
    n_rows = hs * t_new
    for hh in range(hs):
        cols = slice(hh * B_HEAD_DIM, (hh + 1) * B_HEAD_DIM)
        rows = slice(hh * t_new, (hh + 1) * t_new)
        kn = _qk_norm(k_ref[0, :, cols], kg_ref[...], 1.0)
        ko_ref[0, :, cols] = kn
        vo_ref[0, :, cols] = v_ref[0, :, cols]
        qs[rows, :] = _qk_norm(q_ref[0, :, cols], qg_ref[...], B_HEAD_DIM ** -0.5)
        ks[rows, :] = kn
        vs[rows, :] = v_ref[0, :, cols]

    qb = qs[...].astype(BF16)
    s_c = _dot_nt(qb, kc_ref[...].reshape(buf * hs, B_HEAD_DIM).astype(BF16))
    s_n = _dot_nt(qb, ks[...].astype(BF16))
    vcb = vc_ref[...].reshape(buf * hs, B_HEAD_DIM).astype(BF16)
    vnew = vs[...].astype(BF16)

    row_c = lax.broadcasted_iota(jnp.int32, (n_rows, buf * hs), 0)
    col_c = lax.broadcasted_iota(jnp.int32, (n_rows, buf * hs), 1)
    head_r, tok_r = _divmod_const(row_c, t_new)
    cache_row, head_c = _divmod_const(col_c, hs)
    same_c = head_r == head_c
    dist_c = buf + tok_r - cache_row
    row_n = lax.broadcasted_iota(jnp.int32, (n_rows, n_rows), 0)
    col_n = lax.broadcasted_iota(jnp.int32, (n_rows, n_rows), 1)
    head_rn, tok_rn = _divmod_const(row_n, t_new)
    head_cn, tok_cn = _divmod_const(col_n, t_new)
    dist_n = tok_rn - tok_cn
    same_n = jnp.logical_and(head_rn == head_cn, dist_n >= 0)

    outs, lses = [], []
    for win, dil in B_PATTERNS:
        ok_c = jnp.logical_and(same_c, jnp.logical_and((dist_c & (dil - 1)) == 0, dist_c <= win))
        ok_n = jnp.logical_and(same_n, (dist_n & (dil - 1)) == 0)
        sc = jnp.where(ok_c, s_c, NEG_INF)
        sn = jnp.where(ok_n, s_n, NEG_INF)
        m = jnp.maximum(jnp.max(sc, axis=-1, keepdims=True), jnp.max(sn, axis=-1, keepdims=True))
        pc = jnp.exp(sc - m)
        pn = jnp.exp(sn - m)
        den = jnp.sum(pc, axis=-1, keepdims=True) + jnp.sum(pn, axis=-1, keepdims=True)
        num = _dot(pc.astype(BF16), vcb) + _dot(pn.astype(BF16), vnew)
        outs.append(num * (1.0 / den))
        lses.append(m + jnp.log(den))
    mx = jnp.maximum(jnp.maximum(lses[0], lses[1]), lses[2])
    es = [jnp.exp(l - mx) for l in lses]
    out = (es[0] * outs[0] + es[1] * outs[1] + es[2] * outs[2]) * (1.0 / (es[0] + es[1] + es[2]))
    for hh in range(hs):
        o_ref[0, :, hh * B_HEAD_DIM:(hh + 1) * B_HEAD_DIM] = out[hh * t_new:(hh + 1) * t_new, :]


def _attn_sample(qkv, cache_k, cache_v, qg, kg, *, layer):
    n, t_new, _ = qkv.shape
    buf = cache_k.shape[2]
    assert cache_k.shape[3:] == (B_HEADS, B_HEAD_DIM)
    assert all(win <= buf for win, _ in B_PATTERNS) and all(dil & (dil - 1) == 0 for _, dil in B_PATTERNS)
    assert (SAMPLE_HEADS_PER_STEP * t_new) % 8 == 0
    width = B_HEADS * B_HEAD_DIM
    hw = SAMPLE_HEADS_PER_STEP * B_HEAD_DIM
    steps = width // hw
    blk = (1, t_new, hw)

    def in_spec(part):
        return pl.BlockSpec(blk, lambda i, h: (i, 0, part * steps + h))

    c_spec = pl.BlockSpec((None, None, buf, SAMPLE_HEADS_PER_STEP, B_HEAD_DIM), lambda i, h: (layer, i, 0, h, 0))
    g_spec = pl.BlockSpec((1, B_HEAD_DIM), lambda i, h: (0, 0))
    out_spec = pl.BlockSpec(blk, lambda i, h: (i, 0, h))
    return pl.pallas_call(
        functools.partial(_attn_sample_kernel, t_new=t_new, buf=buf),
        grid=(n, steps),
        in_specs=[in_spec(0), in_spec(1), in_spec(2), c_spec, c_spec, g_spec, g_spec],
        out_specs=[out_spec, out_spec, out_spec],
        out_shape=[jax.ShapeDtypeStruct((n, t_new, width), F32)] * 3,
        scratch_shapes=[pltpu.VMEM((SAMPLE_HEADS_PER_STEP * t_new, B_HEAD_DIM), F32)] * 3,
        compiler_params=_params(("parallel", "parallel")),
    )(qkv, qkv, qkv, cache_k, cache_v, qg.reshape(1, -1), kg.reshape(1, -1))


def _split3(x):
    hi = x.astype(BF16)
    r1 = x - hi.astype(F32)
    mid = r1.astype(BF16)
    lo = (r1 - mid.astype(F32)).astype(BF16)
    return hi, mid, lo


def _mlstm_kernel(q_ref, k_ref, v_ref, o_ref, ifc_ref, ifr_ref, bc_ref, br_ref, hg_ref, c0_ref, n0_ref, m0_ref,
                  h_ref, cf_ref, nf_ref, mf_ref, ct_s, n_s, m_s, *, ln):
    j = pl.program_id(1)
    last = pl.num_programs(1) - 1

    @pl.when(j == 0)
    def _():
        for h in range(C_HEADS):
            ct_s[h] = c0_ref[0, h].T
        n_s[...] = n0_ref[0]
        m_s[...] = m0_ref[0]

    ifc = ifc_ref[...] + bc_ref[...]
    ifr = ifr_ref[0] + br_ref[...]
    row = lax.broadcasted_iota(jnp.int32, (ln, ln), 0)
    col = lax.broadcasted_iota(jnp.int32, (ln, ln), 1)
    causal = col <= row
    tri = jnp.where(causal, 1.0, 0.0).astype(BF16)
    triu = jnp.where(row <= col, 1.0, 0.0).astype(BF16)
    b_col = sum(_dot(tri, part) for part in _split3(_log_sigmoid(ifc)))
    b_row = sum(_dot(part, triu) for part in _split3(_log_sigmoid(ifr)))

    for h in range(C_HEADS):
        qc = q_ref[:, h * C_QK_DIM:(h + 1) * C_QK_DIM]
        qb = qc.astype(BF16)
        kc = k_ref[:, h * C_QK_DIM:(h + 1) * C_QK_DIM] * (C_QK_DIM ** -0.5)
        kb = kc.astype(BF16)
        vc = v_ref[:, h * C_V_DIM:(h + 1) * C_V_DIM]
        bcol = b_col[:, C_HEADS + h:C_HEADS + h + 1]
        brow = b_row[C_HEADS + h:C_HEADS + h + 1, :]
        icol = ifc[:, h:h + 1]
        irow = ifr[h:h + 1, :]
        m_prev = m_s[h:h + 1, 0:1]
        n_prev = n_s[h:h + 1, :]
        ct_prev = ct_s[h]

        dmat = jnp.where(causal, bcol - brow + irow, NEG_INF)
        inter = bcol + m_prev
        mt = jnp.maximum(inter, jnp.max(dmat, axis=-1, keepdims=True))
        wmat = jnp.exp(dmat - mt) * _dot_nt(qb, kb)
        w_inter = jnp.exp(inter - mt)
        num = _dot(wmat.astype(BF16), vc.astype(BF16)) + w_inter * _dot(qb, ct_prev.astype(BF16))
        den = jnp.sum(wmat, axis=-1, keepdims=True) + w_inter * jnp.sum(qc * n_prev, axis=-1, keepdims=True)
        hc = num * (1.0 / jnp.maximum(jnp.abs(den), jnp.exp(-mt)))

        m_new = mt[ln - 1:ln, :]
        b_last = bcol[ln - 1:ln, :]
        w_end = jnp.exp(b_last - bcol + icol - m_new)
        decay = jnp.exp(b_last + m_prev - m_new)
        ct_s[h] = decay * ct_prev + _dot(kc.T.astype(BF16), (w_end * vc).astype(BF16))
        n_s[h:h + 1, :] = decay * n_prev + jnp.sum(w_end * kc, axis=0, keepdims=True)
        m_s[h:h + 1, :] = jnp.broadcast_to(m_new, (1, LANES))

        gain = hg_ref[:, h * C_V_DIM:(h + 1) * C_V_DIM]
        hn = hc * lax.rsqrt(jnp.mean(hc * hc, axis=-1, keepdims=True) + EPS) * gain
        og = _sigmoid(o_ref[:, h * C_V_DIM:(h + 1) * C_V_DIM])
        h_ref[:, h * C_V_DIM:(h + 1) * C_V_DIM] = (og * hn).astype(h_ref.dtype)

    @pl.when(j == last)
    def _():
        for h in range(C_HEADS):
            cf_ref[0, h] = ct_s[h].T
        nf_ref[0] = n_s[...]
        mf_ref[0] = m_s[...]


def _mlstm(proj, ifc, ifr, bias_c, bias_r, hn_g, c0, n0, m0, *, n_seq, seq, col_q, col_k, col_v, col_o, ln,
           out_rows):
    ln = min(ln, seq)
    nc = seq // ln
    assert seq % ln == 0 and ln % LANES == 0
    qk_w = C_HEADS * C_QK_DIM
    v_w = C_HEADS * C_V_DIM
    assert col_q % qk_w == 0 and col_k % qk_w == 0 and col_v % v_w == 0 and col_o % v_w == 0
    state_specs = [pl.BlockSpec((1, C_HEADS, C_V_DIM, C_QK_DIM), lambda i, j: (i, 0, 0, 0)),
                   pl.BlockSpec((1, C_HEADS, C_QK_DIM), lambda i, j: (i, 0, 0)),
                   pl.BlockSpec((1, C_HEADS, LANES), lambda i, j: (i, 0, 0))]
    return pl.pallas_call(
        functools.partial(_mlstm_kernel, ln=ln),
        grid=(n_seq, nc),
        in_specs=[pl.BlockSpec((ln, qk_w), lambda i, j: (i * nc + j, col_q // qk_w)),
                  pl.BlockSpec((ln, qk_w), lambda i, j: (i * nc + j, col_k // qk_w)),
                  pl.BlockSpec((ln, v_w), lambda i, j: (i * nc + j, col_v // v_w)),
                  pl.BlockSpec((ln, v_w), lambda i, j: (i * nc + j, col_o // v_w)),
                  pl.BlockSpec((ln, LANES), lambda i, j: (i * nc + j, 0)),
                  pl.BlockSpec((1, 2 * C_HEADS, ln), lambda i, j: (i, 0, j)),
                  pl.BlockSpec((1, LANES), lambda i, j: (0, 0)),
                  pl.BlockSpec((2 * C_HEADS, 1), lambda i, j: (0, 0)),
                  pl.BlockSpec((1, v_w), lambda i, j: (0, 0))] + state_specs,
        out_specs=[pl.BlockSpec((ln, v_w), lambda i, j: (i * nc + j, 0))] + state_specs,
        out_shape=[jax.ShapeDtypeStruct((out_rows, v_w), BF16),
                   jax.ShapeDtypeStruct((n_seq, C_HEADS, C_V_DIM, C_QK_DIM), F32),
                   jax.ShapeDtypeStruct((n_seq, C_HEADS, C_QK_DIM), F32),
                   jax.ShapeDtypeStruct((n_seq, C_HEADS, LANES), F32)],
        scratch_shapes=[pltpu.VMEM((C_HEADS, C_QK_DIM, C_V_DIM), F32),
                        pltpu.VMEM((C_HEADS, C_QK_DIM), F32),
                        pltpu.VMEM((C_HEADS, LANES), F32)],
        compiler_params=_params(("parallel", "arbitrary")),
    )(proj, proj, proj, proj, ifc, ifr, bias_c, bias_r, hn_g.reshape(1, v_w), c0, n0, m0)


MLSTM_CHUNK = 256
GATE_PAD = 1e30
ROW_TILES = 4
MERGE_ROW_TILES = 6
FF2_SPLITS = 4


def _put_rows(full, rows, row0, n_pad):
    block = jnp.pad(rows.astype(full.dtype), ((0, n_pad - rows.shape[0]), (0, 0)))
    return lax.dynamic_update_slice(full, block, (row0, 0))


def kernel(x_prompt, x_sample, cache_swa_k, cache_swa_v, state_mlstm_C, state_mlstm_n, state_mlstm_m, norm1_g, w_in, ws_a, bs_a, norm_va_g, qn_g, kn_g, i_b, f_b, hn_c_g, w_branch_a, w_branch_b, w_branch_c, w_out, norm2_g, w_ff1, w_ff2):
    depth = w_in.shape[0]
    nbp, seq, d = x_prompt.shape
    nbs, t_new, _ = x_sample.shape
    buf = cache_swa_k.shape[2]
    hd = B_HEADS * B_HEAD_DIM
    a_width = d // 2
    qk_w = C_HEADS * C_QK_DIM
    v_w = C_HEADS * C_V_DIM
    n_gate = N_BRANCH * d
    n_main = w_in.shape[2] - n_gate - IF_COLS
    col_b = 2 * a_width
    col_c = col_b + 3 * hd
    assert col_c + 2 * qk_w + 2 * v_w == n_main and seq % A_CHUNK == 0 and t_new <= A_CHUNK

    mp, ms = nbp * seq, nbs * t_new
    quantum = BF16_SUBLANES * math.lcm(ROW_TILES, MERGE_ROW_TILES)
    m_tot = -(-(mp + ms) // quantum) * quantum
    n_tail = m_tot - mp
    tm = m_tot // ROW_TILES
    tm_merge = m_tot // MERGE_ROW_TILES
    tm_norm = m_tot // (ROW_TILES * 3)
    assert mp % n_tail == 0 and n_tail % t_new == 0 and tm_norm % BF16_SUBLANES == 0
    x = jnp.concatenate([x_prompt.reshape(mp, d), x_sample.reshape(ms, d), jnp.zeros((m_tot - mp - ms, d), F32)])

    w_in_t = jnp.swapaxes(w_in, 1, 2)
    zero_state = (jnp.zeros((nbp, C_HEADS, C_V_DIM, C_QK_DIM), F32), jnp.zeros((nbp, C_HEADS, C_QK_DIM), F32),
                  jnp.zeros((nbp, C_HEADS, LANES), F32))
    lane = jnp.arange(LANES)
    gate_pad_row = jnp.where(lane < C_HEADS, -GATE_PAD, jnp.where(lane < IF_COLS, GATE_PAD, 0.0)).astype(F32)
    seqs_tail = n_tail // t_new

    kv_prompt = None
    small = [[] for _ in range(9)]
    for l in range(depth):
        w_if = jnp.pad(w_in_t[l, n_main:n_main + IF_COLS, :], ((0, LANES - IF_COLS), (0, 0)))
        bias = jnp.concatenate([i_b[l], f_b[l]])
        bias_c = jnp.pad(bias, (0, LANES - IF_COLS)).reshape(1, LANES)
        bias_r = bias.reshape(IF_COLS, 1)

        xn = _rmsnorm_bf16(x, norm1_g[l], tm=tm_norm)
        proj = _matmul(xn, w_in_t, layer=l, n_cols=n_main, tm=tm, w_transposed=True)
        ifp = _matmul(xn, w_if, tm=tm, w_transposed=True)
        gates = _matmul(xn, w_in_t, layer=l, n_cols=n_gate, col_off=n_main, row_shift=IF_COLS, tm=tm,
                        w_transposed=True)

        a_out, _ = _gmlp(proj, ws_a[l], bs_a[l].T, norm_va_g[l], width=a_width, chunk=A_CHUNK, chunks=2,
                         row0=0, n_rows=mp, out_rows=m_tot, want_va=False)
        eye = jnp.eye(seqs_tail, dtype=F32)
        ws_tail = jax.vmap(lambda w: jnp.kron(eye, w[:t_new, :t_new]))(ws_a[l])
        bs_tail = jnp.tile(bs_a[l][:, :t_new], (1, seqs_tail)).T
        a_tail, va_tail = _gmlp(proj, ws_tail, bs_tail, norm_va_g[l], width=a_width, chunk=n_tail, chunks=1,
                                row0=mp, n_rows=n_tail, out_rows=n_tail, want_va=True)
        a_out = lax.dynamic_update_slice(a_out, a_tail, (mp, 0))

        b_out, kp, vp = _attn_prompt(proj, qn_g[l], kn_g[l], n_seq=nbp, seq=seq, col0=col_b // B_HEAD_DIM,
                                     out_rows=m_tot, layer=l, depth=depth, kv_prev=kv_prompt)
        kv_prompt = (kp, vp)
        qkv_s = proj[mp:mp + ms, col_b:col_c].reshape(nbs, t_new, 3 * hd)
        b_s, k_s, v_s = _attn_sample(qkv_s, cache_swa_k, cache_swa_v, qn_g[l], kn_g[l], layer=l)
        b_out = _put_rows(b_out, b_s.reshape(ms, hd), mp, n_tail)

        cols = dict(col_q=col_c, col_k=col_c + qk_w, col_v=col_c + 2 * qk_w, col_o=col_c + 2 * qk_w + v_w)
        ifr = jnp.swapaxes(ifp[:mp, :IF_COLS].reshape(nbp, seq, IF_COLS), 1, 2)
        c_out, cp, n_p, m_p = _mlstm(proj, ifp, ifr, bias_c, bias_r, hn_c_g[l], *zero_state, n_seq=nbp, seq=seq,
                                     ln=MLSTM_CHUNK, out_rows=m_tot, **cols)
        pad = A_CHUNK - t_new
        proj_s = jnp.pad(proj[mp:mp + ms, col_c:].reshape(nbs, t_new, n_main - col_c), ((0, 0), (0, pad), (0, 0)))
        ifc_s = jnp.concatenate([ifp[mp:mp + ms].reshape(nbs, t_new, LANES),
                                 jnp.broadcast_to(gate_pad_row, (nbs, pad, LANES))], axis=1)
        ifr_s = jnp.swapaxes(ifc_s[:, :, :IF_COLS], 1, 2)
        m0 = jnp.broadcast_to(state_mlstm_m[l][..., None], (nbs, C_HEADS, LANES))
        c_s_out, c_s, n_s, m_s = _mlstm(
            proj_s.reshape(nbs * A_CHUNK, -1), ifc_s.reshape(nbs * A_CHUNK, LANES), ifr_s, bias_c, bias_r, hn_c_g[l],
            state_mlstm_C[l], state_mlstm_n[l], m0, n_seq=nbs, seq=A_CHUNK, ln=A_CHUNK, out_rows=nbs * A_CHUNK,
            col_q=0, col_k=qk_w, col_v=2 * qk_w, col_o=2 * qk_w + v_w)
        c_out = _put_rows(c_out, c_s_out.reshape(nbs, A_CHUNK, v_w)[:, :t_new].reshape(ms, v_w), mp, n_tail)

        merged = _merge(a_out, b_out, c_out, w_branch_a, w_branch_b, w_branch_c, gates, layer=l, tm=tm_merge)
        x = _matmul(merged, w_out, layer=l, epilogue="resid", resid=x, tm=tm, vmem_limit_bytes=WIDE_VMEM_LIMIT_BYTES)
        h = _matmul(_rmsnorm_bf16(x, norm2_g[l], tm=tm_norm), w_ff1, layer=l, epilogue="relu2", out_dtype=BF16, tm=tm)
        k_len = w_ff2.shape[1] // FF2_SPLITS
        for part in range(FF2_SPLITS):
            final = l == depth - 1 and part == FF2_SPLITS - 1
            x = _matmul(h, w_ff2, layer=l, k_off=part * k_len, k_len=k_len, epilogue="resid", resid=x,
                        tm=tm, main_rows=mp if final else None, vmem_limit_bytes=WIDE_VMEM_LIMIT_BYTES)

        vals = (cp, n_p, m_p[..., 0], k_s.reshape(nbs, t_new, B_HEADS, B_HEAD_DIM),
                v_s.reshape(nbs, t_new, B_HEADS, B_HEAD_DIM), c_s, n_s, m_s[..., 0],
                va_tail[:ms].reshape(nbs, t_new, a_width))
        for o, v in zip(small, vals):
            o.append(v)

    keep = min(max(w for w, _ in B_PATTERNS), seq)
    kp, vp = (t[:, :, seq - keep:].reshape(depth, nbp, keep, B_HEADS, B_HEAD_DIM) for t in kv_prompt)
    st = [jnp.stack(o) for o in small]
    y_prompt, y_tail = x
    return (y_prompt.reshape(nbp, seq, d), y_tail[:ms].reshape(nbs, t_new, d), kp, vp, st[0], st[1], st[2],
            st[3], st[4], st[5], st[6], st[7], st[8])
```

```python
import functools
import math

import jax
import jax.numpy as jnp
from jax import lax
from jax.experimental import pallas as pl
from jax.experimental.pallas import tpu as pltpu

F32 = jnp.float32
BF16 = jnp.bfloat16

EPS = 1e-6
NEG_INF = -1e30

A_CHUNK = 128
A_GROUPS = 8
B_HEADS = 16
B_HEAD_DIM = 128
B_PATTERNS = ((128, 1), (512, 4), (2048, 16))
B_BLOCK = 128
C_HEADS = 8
C_QK_DIM = 128
C_V_DIM = 256
N_BRANCH = 3
IF_COLS = 2 * C_HEADS

V7X_VMEM_BYTES = 64 * 1024 * 1024
VMEM_LIMIT_BYTES = V7X_VMEM_BYTES - 8 * 1024 * 1024
WIDE_VMEM_LIMIT_BYTES = V7X_VMEM_BYTES - 3 * 1024 * 1024
LANES = 128
BF16_SUBLANES = 16
SINGLE_BUFFER_BYTES = 4 << 20


def _params(semantics, vmem_limit_bytes=VMEM_LIMIT_BYTES):
    return pltpu.CompilerParams(dimension_semantics=semantics, vmem_limit_bytes=vmem_limit_bytes)


def _gelu(x):
    c = math.sqrt(2.0 / math.pi)
    return 0.5 * x * (1.0 + jnp.tanh(c * (x + 0.044715 * (x * x * x))))


def _sigmoid(x):
    return 1.0 / (1.0 + jnp.exp(-x))


def _log_sigmoid(x):
    return -(jnp.maximum(-x, 0.0) + jnp.log(1.0 + jnp.exp(-jnp.abs(x))))


def _dot(a, b):
    return jnp.dot(a, b, preferred_element_type=F32)


def _dot_nt(a, b):
    return lax.dot_general(a, b, (((1,), (1,)), ((), ())), preferred_element_type=F32)


def _rmsnorm_kernel(x_ref, g_ref, o_ref):
    x = x_ref[...]
    y = x * lax.rsqrt(jnp.mean(x * x, axis=-1, keepdims=True) + EPS)
    o_ref[...] = (y * g_ref[...]).astype(o_ref.dtype)


def _rmsnorm_bf16(x, g, *, tm):
    m, d = x.shape
    assert m % tm == 0
    return pl.pallas_call(
        _rmsnorm_kernel,
        grid=(m // tm,),
        in_specs=[pl.BlockSpec((tm, d), lambda i: (i, 0)), pl.BlockSpec((1, d), lambda i: (0, 0))],
        out_specs=pl.BlockSpec((tm, d), lambda i: (i, 0)),
        out_shape=jax.ShapeDtypeStruct((m, d), BF16),
        compiler_params=_params(("parallel",)),
    )(x, g.reshape(1, d))


def _mm_kernel(x_ref, w_ref, *rest, epilogue, w_transposed, tail):
    n_out = 2 if tail else 1
    outs = rest[-n_out:]
    extra = rest[:-n_out]
    w = (w_ref[0] if len(w_ref.shape) == 3 else w_ref[...]).astype(BF16)
    acc = _dot_nt(x_ref[...], w) if w_transposed else _dot(x_ref[...], w)
    if epilogue == "relu2":
        r = jnp.maximum(acc, 0.0)
        acc = r * r
    elif epilogue == "resid":
        acc = acc + extra[-1][...]
    outs[0][...] = acc.astype(outs[0].dtype)
    if tail:
        lo, n_tail = tail
        outs[1][...] = acc[lo:lo + n_tail, :].astype(outs[1].dtype)


def _matmul(x, w, *, tm, tn=512, layer=None, n_cols=None, col_off=0, k_off=0, k_len=None, epilogue="none",
            resid=None, out_dtype=F32, w_transposed=False, row_shift=0, main_rows=None,
            vmem_limit_bytes=VMEM_LIMIT_BYTES):
    m, kdim = x.shape
    k_len = kdim if k_len is None else k_len
    n = w.shape[-2 if w_transposed else -1] if n_cols is None else n_cols
    tn = min(tn, n)
    assert m % tm == 0 and n % tn == 0 and k_off % k_len == 0 and col_off % tn == 0
    assert row_shift == 0 or (w_transposed and row_shift % 8 == 0)
    joff, kblk = col_off // tn, k_off // k_len
    x_mode = pl.Buffered(1) if (n // tn > 1 and tm * k_len * 2 > SINGLE_BUFFER_BYTES) else None

    if row_shift:
        assert layer is not None
        w_spec = pl.BlockSpec((pl.Element(1), pl.Element(tn), pl.Element(k_len)),
                              lambda i, j: (layer, ((col_off + row_shift) // 8 + j * (tn // 8)) * 8, k_off))
    else:
        if w_transposed:
            w_blk, w_idx = (tn, k_len), lambda i, j: (j + joff, kblk)
        else:
            w_blk, w_idx = (k_len, tn), lambda i, j: (kblk, j + joff)
        if layer is None:
            w_spec = pl.BlockSpec(w_blk, w_idx)
        else:
            w_spec = pl.BlockSpec((None,) + w_blk, lambda i, j: (layer,) + w_idx(i, j))
    in_specs = [pl.BlockSpec((tm, k_len), lambda i, j: (i, kblk), pipeline_mode=x_mode), w_spec]
    args = [x, w]
    if epilogue == "resid":
        in_specs.append(pl.BlockSpec((tm, tn), lambda i, j: (i, j)))
        args.append(resid)
    out_specs = pl.BlockSpec((tm, tn), lambda i, j: (i, j))
    out_shape = jax.ShapeDtypeStruct((m, n), out_dtype)
    tail = None
    if main_rows is not None:
        n_tail = m - main_rows
        lo = main_rows - (m // tm - 1) * tm
        assert 0 <= lo and lo % 8 == 0 and lo + n_tail == tm
        tail = (lo, n_tail)
        out_specs = [out_specs, pl.BlockSpec((n_tail, tn), lambda i, j: (i, j))]
        out_shape = [jax.ShapeDtypeStruct((main_rows, n), out_dtype),
                     jax.ShapeDtypeStruct((m // tm * n_tail, n), out_dtype)]
    res = pl.pallas_call(
        functools.partial(_mm_kernel, epilogue=epilogue, w_transposed=w_transposed, tail=tail),
        grid=(m // tm, n // tn),
        in_specs=in_specs,
        out_specs=out_specs,
        out_shape=out_shape,
        compiler_params=_params(("parallel", "parallel"), vmem_limit_bytes),
    )(*args)
    if tail:
        return res[0], res[1][(m // tm - 1) * tail[1]:]
    return res


def _merge_kernel(a_ref, b_ref, c_ref, wa_ref, wb_ref, wc_ref, ga_ref, gb_ref, gc_ref, o_ref):
    acc = _sigmoid(ga_ref[...]) * _dot(a_ref[...], wa_ref[...].astype(BF16))
    acc += _sigmoid(gb_ref[...]) * _dot(b_ref[...], wb_ref[...].astype(BF16))
    acc += _sigmoid(gc_ref[...]) * _dot(c_ref[...], wc_ref[...].astype(BF16))
    o_ref[...] = acc.astype(o_ref.dtype)


def _merge(a, b, c, wa, wb, wc, gates, *, layer, tm, tn=256):
    m, kdim = a.shape
    n = wa.shape[-1]
    assert m % tm == 0 and n % tn == 0 and gates.shape == (m, N_BRANCH * n)
    nj = n // tn
    x_mode = pl.Buffered(1) if tm * kdim * 2 > SINGLE_BUFFER_BYTES else None

    def x_spec():
        return pl.BlockSpec((tm, kdim), lambda i, j: (i, 0), pipeline_mode=x_mode)

    def g_spec(branch):
        return pl.BlockSpec((tm, tn), lambda i, j: (i, j + branch * nj))

    w_spec = pl.BlockSpec((None, kdim, tn), lambda i, j: (layer, 0, j))
    return pl.pallas_call(
        _merge_kernel,
        grid=(m // tm, nj),
        in_specs=[x_spec(), x_spec(), x_spec(), w_spec, w_spec, w_spec, g_spec(0), g_spec(1), g_spec(2)],
        out_specs=pl.BlockSpec((tm, tn), lambda i, j: (i, j)),
        out_shape=jax.ShapeDtypeStruct((m, n), BF16),
        compiler_params=_params(("parallel", "parallel")),
    )(a, b, c, wa, wb, wc, gates, gates, gates)


def _gmlp_kernel(u_ref, v_ref, w_ref, b_ref, g_ref, a_ref, *va_refs, chunk, chunks, gdim):
    row = lax.broadcasted_iota(jnp.int32, (chunk, chunk), 0)
    col = lax.broadcasted_iota(jnp.int32, (chunk, chunk), 1)
    causal = col <= row
    w_groups = [jnp.where(causal, w_ref[g], 0.0).astype(BF16) for g in range(A_GROUPS)]
    gain = g_ref[...]
    bias = b_ref[...]
    for c in range(chunks):
        rows = pl.ds(c * chunk, chunk)
        v = _gelu(v_ref[rows, :])
        vn = v * lax.rsqrt(jnp.mean(v * v, axis=-1, keepdims=True) + EPS) * gain
        if va_refs:
            va_refs[0][rows, :] = vn
        vb = vn.astype(BF16)
        for g in range(A_GROUPS):
            cols = slice(g * gdim, (g + 1) * gdim)
            mixed = _dot(w_groups[g], vb[:, cols]) + bias[:, g:g + 1]
            u = _gelu(u_ref[rows, cols])
            a_ref[rows, cols] = (u * mixed).astype(a_ref.dtype)


def _gmlp(proj, ws, bs_t, norm_g, *, width, chunk, chunks, row0, n_rows, out_rows, want_va):
    rows = chunk * chunks
    assert n_rows % rows == 0 and row0 % rows == 0
    blk0 = row0 // rows
    gdim = width // A_GROUPS
    out_shape = [jax.ShapeDtypeStruct((out_rows, width), BF16)]
    out_specs = [pl.BlockSpec((rows, width), lambda j: (j, 0))]
    if want_va:
        out_shape.append(jax.ShapeDtypeStruct((out_rows, width), F32))
        out_specs.append(pl.BlockSpec((rows, width), lambda j: (j, 0)))
    res = pl.pallas_call(
        functools.partial(_gmlp_kernel, chunk=chunk, chunks=chunks, gdim=gdim),
        grid=(n_rows // rows,),
        in_specs=[pl.BlockSpec((rows, width), lambda j: (blk0 + j, 0)),
                  pl.BlockSpec((rows, width), lambda j: (blk0 + j, 1)),
                  pl.BlockSpec((A_GROUPS, chunk, chunk), lambda j: (0, 0, 0)),
                  pl.BlockSpec((chunk, A_GROUPS), lambda j: (0, 0)),
                  pl.BlockSpec((1, width), lambda j: (0, 0))],
        out_specs=out_specs,
        out_shape=out_shape,
        compiler_params=_params(("parallel",)),
    )(proj, proj, ws, bs_t, norm_g.reshape(1, width))
    return res if want_va else (res[0], None)


def _qk_norm(x, g, scale):
    return x * lax.rsqrt(jnp.mean(x * x, axis=-1, keepdims=True) + EPS) * (g * scale)


ATTN_GROUP = 4


ATTN_HEADS_PER_STEP = 2


def _attn_prompt_kernel(*refs, seq, aliased):
    q_ref, k_ref, v_ref, qg_ref, kg_ref = refs[:5]
    o_ref, ko_ref, vo_ref = refs[5 + (2 if aliased else 0):][:3]
    scratch = refs[5 + (2 if aliased else 0) + 3:]
    per_head = len(scratch) // ATTN_HEADS_PER_STEP
    for hh in range(ATTN_HEADS_PER_STEP):
        cols = pl.ds(hh * B_HEAD_DIM, B_HEAD_DIM)
        _attn_prompt_head(q_ref.at[:, cols], k_ref.at[:, cols], v_ref.at[:, cols], qg_ref, kg_ref,
                          o_ref.at[:, cols], ko_ref.at[:, cols], vo_ref.at[:, cols],
                          *scratch[hh * per_head:(hh + 1) * per_head], seq=seq)


def _attn_prompt_head(q_ref, k_ref, v_ref, qg_ref, kg_ref, o_ref, ko_ref, vo_ref, qp, kp, vp, os_, ls, sq, sk, sv,
                      *, seq):
    v = v_ref[...]
    kn = _qk_norm(k_ref[...], kg_ref[...], 1.0)
    qn = _qk_norm(q_ref[...], qg_ref[...], B_HEAD_DIM ** -0.5)
    ko_ref[...] = kn
    vo_ref[...] = v
    data = pl.ds(B_BLOCK, seq)
    for p in range(len(B_PATTERNS)):
        kp[p, 0:B_BLOCK, :] = jnp.zeros((B_BLOCK, B_HEAD_DIM), BF16)
        vp[p, 0:B_BLOCK, :] = jnp.zeros((B_BLOCK, 2 * B_HEAD_DIM), BF16)
        vp[p, data, B_HEAD_DIM:] = jnp.ones((seq, B_HEAD_DIM), BF16)
    qp[0] = qn.astype(BF16)
    kp[0, data, :] = kn.astype(BF16)
    vp[0, data, 0:B_HEAD_DIM] = v.astype(BF16)
    sq[...] = qn
    sk[...] = kn
    sv[...] = v
    for p, (win, dil) in enumerate(B_PATTERNS):
        if dil == 1:
            continue
        ln = seq // dil
        for r in range(dil):
            src = pl.ds(r, ln, stride=dil)
            qp[p, pl.ds(r * ln, ln), :] = sq[src, :].astype(BF16)
            kp[p, pl.ds(B_BLOCK + r * ln, ln), :] = sk[src, :].astype(BF16)
            vp[p, pl.ds(B_BLOCK + r * ln, ln), 0:B_HEAD_DIM] = sv[src, :].astype(BF16)

    qi = lax.broadcasted_iota(jnp.int32, (B_BLOCK, B_BLOCK), 0)
    kj = lax.broadcasted_iota(jnp.int32, (B_BLOCK, B_BLOCK), 1)
    mask_cur = kj <= qi
    mask_prev = kj >= qi

    def group(p, dil, g0, firsts, position0):
        n_q = len(firsts) * B_BLOCK
        base = g0 * B_BLOCK
        if not isinstance(base, int):
            base = pl.multiple_of(base, B_BLOCK)
        s = _dot_nt(qp[p, pl.ds(base, n_q), :], kp[p, pl.ds(base, n_q + B_BLOCK), :])
        for u, first in enumerate(firsts):
            lo, mid, hi = u * B_BLOCK, (u + 1) * B_BLOCK, (u + 2) * B_BLOCK
            s_cur = jnp.where(mask_cur, s[lo:mid, mid:hi], NEG_INF)
            if first:
                m = jnp.max(s_cur, axis=-1, keepdims=True)
                res = _dot(jnp.exp(s_cur - m).astype(BF16), vp[p, pl.ds(base + mid, B_BLOCK), :])
            else:
                s_prev = jnp.where(mask_prev, s[lo:mid, lo:mid], NEG_INF)
                m = jnp.max(jnp.maximum(s_cur, s_prev), axis=-1, keepdims=True)
                probs = jnp.concatenate([jnp.exp(s_prev - m).astype(BF16), jnp.exp(s_cur - m).astype(BF16)], axis=1)
                res = _dot(probs, vp[p, pl.ds(base + lo, 2 * B_BLOCK), :])
            den = res[:, B_HEAD_DIM:]
            rows = pl.ds(position0(u), B_BLOCK, stride=dil) if dil > 1 else pl.ds(base + lo, B_BLOCK)
            os_[p, rows, :] = res[:, :B_HEAD_DIM] * (1.0 / den)
            ls[p, rows, :] = m + jnp.log(den)

    for p, (win, dil) in enumerate(B_PATTERNS):
        nb = seq // (dil * B_BLOCK)
        if nb == 1:
            def body(t, carry, p=p, dil=dil):
                group(p, dil, t * ATTN_GROUP, (True,) * ATTN_GROUP, lambda u: t * ATTN_GROUP + u)
                return carry

            lax.fori_loop(0, dil // ATTN_GROUP, body, 0, unroll=True)
            continue
        head = (True,) + (False,) * (ATTN_GROUP - 1)
        tail = (False,) * ATTN_GROUP
        per_class = nb // ATTN_GROUP

        def one_class(r, carry, p=p, dil=dil, nb=nb, per_class=per_class, head=head, tail=tail):
            group(p, dil, r * nb, head, lambda u: u * (B_BLOCK * dil) + r)
            if per_class > 1:
                def rest(t, c2):
                    group(p, dil, r * nb + t * ATTN_GROUP, tail,
                          lambda u: (t * ATTN_GROUP + u) * (B_BLOCK * dil) + r)
                    return c2

                lax.fori_loop(1, per_class, rest, 0, unroll=True)
            return carry

        if dil == 1:
            one_class(0, 0)
        else:
            lax.fori_loop(0, dil, one_class, 0, unroll=True)

    l0, l1, l2 = ls[0], ls[1], ls[2]
    mx = jnp.maximum(jnp.maximum(l0, l1), l2)
    e0, e1, e2 = jnp.exp(l0 - mx), jnp.exp(l1 - mx), jnp.exp(l2 - mx)
    out = (e0 * os_[0] + e1 * os_[1] + e2 * os_[2]) * (1.0 / (e0 + e1 + e2))
    o_ref[...] = out.astype(o_ref.dtype)


def _attn_prompt(proj, qg, kg, *, n_seq, seq, col0, out_rows, layer, depth, kv_prev=None):
    assert B_PATTERNS[0][1] == 1 and len(B_PATTERNS) == 3
    for _, dil in B_PATTERNS:
        nb = seq // (dil * B_BLOCK)
        assert seq % (dil * B_BLOCK) == 0 and (nb % ATTN_GROUP == 0 or (nb == 1 and dil % ATTN_GROUP == 0))
    width = B_HEADS * B_HEAD_DIM
    hp = ATTN_HEADS_PER_STEP
    assert B_HEADS % hp == 0 and col0 % hp == 0
    blk = (seq, hp * B_HEAD_DIM)

    def in_spec(part):
        return pl.BlockSpec(blk, lambda i, h: (i, (col0 + part * B_HEADS) // hp + h))

    g_spec = pl.BlockSpec((1, B_HEAD_DIM), lambda i, h: (0, 0))
    kv_spec = pl.BlockSpec((None, None, seq, hp * B_HEAD_DIM), lambda i, h: (layer, i, 0, h))
    kv_shape = jax.ShapeDtypeStruct((depth, n_seq, seq, width), F32)
    in_specs = [in_spec(0), in_spec(1), in_spec(2), g_spec, g_spec]
    args = [proj, proj, proj, qg.reshape(1, -1), kg.reshape(1, -1)]
    aliases = {}
    if kv_prev is not None:
        in_specs += [pl.BlockSpec(memory_space=pl.ANY)] * 2
        args += list(kv_prev)
        aliases = {5: 1, 6: 2}
    n_pat = len(B_PATTERNS)
    return pl.pallas_call(
        functools.partial(_attn_prompt_kernel, seq=seq, aliased=kv_prev is not None),
        grid=(n_seq, B_HEADS // hp),
        in_specs=in_specs,
        out_specs=[pl.BlockSpec(blk, lambda i, h: (i, h)), kv_spec, kv_spec],
        out_shape=[jax.ShapeDtypeStruct((out_rows, width), BF16), kv_shape, kv_shape],
        input_output_aliases=aliases,
        scratch_shapes=([pltpu.VMEM((n_pat, seq, B_HEAD_DIM), BF16),
                         pltpu.VMEM((n_pat, seq + B_BLOCK, B_HEAD_DIM), BF16),
                         pltpu.VMEM((n_pat, seq + B_BLOCK, 2 * B_HEAD_DIM), BF16)]
                        + [pltpu.VMEM((n_pat, seq, B_HEAD_DIM), F32)] * 2
                        + [pltpu.VMEM((seq, B_HEAD_DIM), F32)] * 3) * hp,
        compiler_params=_params(("parallel", "parallel")),
    )(*args)


SAMPLE_HEADS_PER_STEP = 8


def _divmod_const(x, d):
    if d & (d - 1) == 0:
        return x >> (d.bit_length() - 1), x & (d - 1)
    return x // d, x % d


def _attn_sample_kernel(q_ref, k_ref, v_ref, kc_ref, vc_ref, qg_ref, kg_ref, o_ref, ko_ref, vo_ref,
                        qs, ks, vs, *, t_new, buf):
    hs = SAMPLE_HEADS_PER_STEP
    n_rows = hs * t_new
    for hh in range(hs):
        cols = slice(hh * B_HEAD_DIM, (hh + 1) * B_HEAD_DIM)
        rows = slice(hh * t_new, (hh + 1) * t_new)
        kn = _qk_norm(k_ref[0, :, cols], kg_ref[...], 1.0)
        ko_ref[0, :, cols] = kn
        vo_ref[0, :, cols] = v_ref[0, :, cols]
        qs[rows, :] = _qk_norm(q_ref[0, :, cols], qg_ref[...], B_HEAD_DIM ** -0.5)
        ks[rows, :] = kn
        vs[rows, :] = v_ref[0, :, cols]

    qb = qs[...].astype(BF16)
    s_c = _dot_nt(qb, kc_ref[...].reshape(buf * hs, B_HEAD_DIM).astype(BF16))
    s_n = _dot_nt(qb, ks[...].astype(BF16))
    vcb = vc_ref[...].reshape(buf * hs, B_HEAD_DIM).astype(BF16)
    vnew = vs[...].astype(BF16)

    row_c = lax.broadcasted_iota(jnp.int32, (n_rows, buf * hs), 0)
    col_c = lax.broadcasted_iota(jnp.int32, (n_rows, buf * hs), 1)
    head_r, tok_r = _divmod_const(row_c, t_new)
    cache_row, head_c = _divmod_const(col_c, hs)
    same_c = head_r == head_c
    dist_c = buf + tok_r - cache_row
    row_n = lax.broadcasted_iota(jnp.int32, (n_rows, n_rows), 0)
    col_n = lax.broadcasted_iota(jnp.int32, (n_rows, n_rows), 1)
    head_rn, tok_rn = _divmod_const(row_n, t_new)
    head_cn, tok_cn = _divmod_const(col_n, t_new)
    dist_n = tok_rn - tok_cn
    same_n = jnp.logical_and(head_rn == head_cn, dist_n >= 0)

    outs, lses = [], []
    for win, dil in B_PATTERNS:
        ok_c = jnp.logical_and(same_c, jnp.logical_and((dist_c & (dil - 1)) == 0, dist_c <= win))
        ok_n = jnp.logical_and(same_n, (dist_n & (dil - 1)) == 0)
        sc = jnp.where(ok_c, s_c, NEG_INF)
        sn = jnp.where(ok_n, s_n, NEG_INF)
        m = jnp.maximum(jnp.max(sc, axis=-1, keepdims=True), jnp.max(sn, axis=-1, keepdims=True))
        pc = jnp.exp(sc - m)
        pn = jnp.exp(sn - m)
        den = jnp.sum(pc, axis=-1, keepdims=True) + jnp.sum(pn, axis=-1, keepdims=True)
        num = _dot(pc.astype(BF16), vcb) + _dot(pn.astype(BF16), vnew)
        outs.append(num * (1.0 / den))
        lses.append(m + jnp.log(den))
    mx = jnp.maximum(jnp.maximum(lses[0], lses[1]), lses[2])
    es = [jnp.exp(l - mx) for l in lses]
    out = (es[0] * outs[0] + es[1] * outs[1] + es[2] * outs[2]) * (1.0 / (es[0] + es[1] + es[2]))
    for hh in range(hs):
        o_ref[0, :, hh * B_HEAD_DIM:(hh + 1) * B_HEAD_DIM] = out[hh * t_new:(hh + 1) * t_new, :]


def _attn_sample(qkv, cache_k, cache_v, qg, kg, *, layer):
    n, t_new, _ = qkv.shape
    buf = cache_k.shape[2]
    assert cache_k.shape[3:] == (B_HEADS, B_HEAD_DIM)
    assert all(win <= buf for win, _ in B_PATTERNS) and all(dil & (dil - 1) == 0 for _, dil in B_PATTERNS)
    assert (SAMPLE_HEADS_PER_STEP * t_new) % 8 == 0
    width = B_HEADS * B_HEAD_DIM
    hw = SAMPLE_HEADS_PER_STEP * B_HEAD_DIM
    steps = width // hw
    blk = (1, t_new, hw)

    def in_spec(part):
        return pl.BlockSpec(blk, lambda i, h: (i, 0, part * steps + h))

    c_spec = pl.BlockSpec((None, None, buf, SAMPLE_HEADS_PER_STEP, B_HEAD_DIM), lambda i, h: (layer, i, 0, h, 0))
    g_spec = pl.BlockSpec((1, B_HEAD_DIM), lambda i, h: (0, 0))
    out_spec = pl.BlockSpec(blk, lambda i, h: (i, 0, h))
    return pl.pallas_call(
        functools.partial(_attn_sample_kernel, t_new=t_new, buf=buf),
        grid=(n, steps),
        in_specs=[in_spec(0), in_spec(1), in_spec(2), c_spec, c_spec, g_spec, g_spec],
        out_specs=[out_spec, out_spec, out_spec],
        out_shape=[jax.ShapeDtypeStruct((n, t_new, width), F32)] * 3,
        scratch_shapes=[pltpu.VMEM((SAMPLE_HEADS_PER_STEP * t_new, B_HEAD_DIM), F32)] * 3,
        compiler_params=_params(("parallel", "parallel")),
    )(qkv, qkv, qkv, cache_k, cache_v, qg.reshape(1, -1), kg.reshape(1, -1))


def _split3(x):
    hi = x.astype(BF16)
    r1 = x - hi.astype(F32)
    mid = r1.astype(BF16)
    lo = (r1 - mid.astype(F32)).astype(BF16)
    return hi, mid, lo


def _mlstm_kernel(q_ref, k_ref, v_ref, o_ref, ifc_ref, ifr_ref, bc_ref, br_ref, hg_ref, c0_ref, n0_ref, m0_ref,
                  h_ref, cf_ref, nf_ref, mf_ref, ct_s, n_s, m_s, *, ln):
    j = pl.program_id(1)
    last = pl.num_programs(1) - 1

    @pl.when(j == 0)
    def _():
        for h in range(C_HEADS):
            ct_s[h] = c0_ref[0, h].T
        n_s[...] = n0_ref[0]
        m_s[...] = m0_ref[0]

    ifc = ifc_ref[...] + bc_ref[...]
    ifr = ifr_ref[0] + br_ref[...]
    row = lax.broadcasted_iota(jnp.int32, (ln, ln), 0)
    col = lax.broadcasted_iota(jnp.int32, (ln, ln), 1)
    causal = col <= row
    tri = jnp.where(causal, 1.0, 0.0).astype(BF16)
    triu = jnp.where(row <= col, 1.0, 0.0).astype(BF16)
    b_col = sum(_dot(tri, part) for part in _split3(_log_sigmoid(ifc)))
    b_row = sum(_dot(part, triu) for part in _split3(_log_sigmoid(ifr)))

    for h in range(C_HEADS):
        qc = q_ref[:, h * C_QK_DIM:(h + 1) * C_QK_DIM]
        qb = qc.astype(BF16)
        kc = k_ref[:, h * C_QK_DIM:(h + 1) * C_QK_DIM] * (C_QK_DIM ** -0.5)
        kb = kc.astype(BF16)
        vc = v_ref[:, h * C_V_DIM:(h + 1) * C_V_DIM]
        bcol = b_col[:, C_HEADS + h:C_HEADS + h + 1]
        brow = b_row[C_HEADS + h:C_HEADS + h + 1, :]
        icol = ifc[:, h:h + 1]
        irow = ifr[h:h + 1, :]
        m_prev = m_s[h:h + 1, 0:1]
        n_prev = n_s[h:h + 1, :]
        ct_prev = ct_s[h]

        dmat = jnp.where(causal, bcol - brow + irow, NEG_INF)
        inter = bcol + m_prev
        mt = jnp.maximum(inter, jnp.max(dmat, axis=-1, keepdims=True))
        wmat = jnp.exp(dmat - mt) * _dot_nt(qb, kb)
        w_inter = jnp.exp(inter - mt)
        num = _dot(wmat.astype(BF16), vc.astype(BF16)) + w_inter * _dot(qb, ct_prev.astype(BF16))
        den = jnp.sum(wmat, axis=-1, keepdims=True) + w_inter * jnp.sum(qc * n_prev, axis=-1, keepdims=True)
        hc = num * (1.0 / jnp.maximum(jnp.abs(den), jnp.exp(-mt)))

        m_new = mt[ln - 1:ln, :]
        b_last = bcol[ln - 1:ln, :]
        w_end = jnp.exp(b_last - bcol + icol - m_new)
        decay = jnp.exp(b_last + m_prev - m_new)
        ct_s[h] = decay * ct_prev + _dot(kc.T.astype(BF16), (w_end * vc).astype(BF16))
        n_s[h:h + 1, :] = decay * n_prev + jnp.sum(w_end * kc, axis=0, keepdims=True)
        m_s[h:h + 1, :] = jnp.broadcast_to(m_new, (1, LANES))

        gain = hg_ref[:, h * C_V_DIM:(h + 1) * C_V_DIM]
        hn = hc * lax.rsqrt(jnp.mean(hc * hc, axis=-1, keepdims=True) + EPS) * gain
        og = _sigmoid(o_ref[:, h * C_V_DIM:(h + 1) * C_V_DIM])
        h_ref[:, h * C_V_DIM:(h + 1) * C_V_DIM] = (og * hn).astype(h_ref.dtype)

    @pl.when(j == last)
    def _():
        for h in range(C_HEADS):
            cf_ref[0, h] = ct_s[h].T
        nf_ref[0] = n_s[...]
        mf_ref[0] = m_s[...]


def _mlstm(proj, ifc, ifr, bias_c, bias_r, hn_g, c0, n0, m0, *, n_seq, seq, col_q, col_k, col_v, col_o, ln,
           out_rows):
    ln = min(ln, seq)
    nc = seq // ln
    assert seq % ln == 0 and ln % LANES == 0
    qk_w = C_HEADS * C_QK_DIM
    v_w = C_HEADS * C_V_DIM
    assert col_q % qk_w == 0 and col_k % qk_w == 0 and col_v % v_w == 0 and col_o % v_w == 0
    state_specs = [pl.BlockSpec((1, C_HEADS, C_V_DIM, C_QK_DIM), lambda i, j: (i, 0, 0, 0)),
                   pl.BlockSpec((1, C_HEADS, C_QK_DIM), lambda i, j: (i, 0, 0)),
                   pl.BlockSpec((1, C_HEADS, LANES), lambda i, j: (i, 0, 0))]
    return pl.pallas_call(
        functools.partial(_mlstm_kernel, ln=ln),
        grid=(n_seq, nc),
        in_specs=[pl.BlockSpec((ln, qk_w), lambda i, j: (i * nc + j, col_q // qk_w)),
                  pl.BlockSpec((ln, qk_w), lambda i, j: (i * nc + j, col_k // qk_w)),
                  pl.BlockSpec((ln, v_w), lambda i, j: (i * nc + j, col_v // v_w)),
                  pl.BlockSpec((ln, v_w), lambda i, j: (i * nc + j, col_o // v_w)),
                  pl.BlockSpec((ln, LANES), lambda i, j: (i * nc + j, 0)),
                  pl.BlockSpec((1, 2 * C_HEADS, ln), lambda i, j: (i, 0, j)),
                  pl.BlockSpec((1, LANES), lambda i, j: (0, 0)),
                  pl.BlockSpec((2 * C_HEADS, 1), lambda i, j: (0, 0)),
                  pl.BlockSpec((1, v_w), lambda i, j: (0, 0))] + state_specs,
        out_specs=[pl.BlockSpec((ln, v_w), lambda i, j: (i * nc + j, 0))] + state_specs,
        out_shape=[jax.ShapeDtypeStruct((out_rows, v_w), BF16),
                   jax.ShapeDtypeStruct((n_seq, C_HEADS, C_V_DIM, C_QK_DIM), F32),
                   jax.ShapeDtypeStruct((n_seq, C_HEADS, C_QK_DIM), F32),
                   jax.ShapeDtypeStruct((n_seq, C_HEADS, LANES), F32)],
        scratch_shapes=[pltpu.VMEM((C_HEADS, C_QK_DIM, C_V_DIM), F32),
                        pltpu.VMEM((C_HEADS, C_QK_DIM), F32),
                        pltpu.VMEM((C_HEADS, LANES), F32)],
        compiler_params=_params(("parallel", "arbitrary")),
    )(proj, proj, proj, proj, ifc, ifr, bias_c, bias_r, hn_g.reshape(1, v_w), c0, n0, m0)


MLSTM_CHUNK = 256
GATE_PAD = 1e30
ROW_TILES = 4
MERGE_ROW_TILES = 6
FF2_SPLITS = 4


def _put_rows(full, rows, row0, n_pad):
    block = jnp.pad(rows.astype(full.dtype), ((0, n_pad - rows.shape[0]), (0, 0)))
    return lax.dynamic_update_slice(full, block, (row0, 0))


def kernel(x_prompt, x_sample, cache_swa_k, cache_swa_v, state_mlstm_C, state_mlstm_n, state_mlstm_m, norm1_g, w_in, ws_a, bs_a, norm_va_g, qn_g, kn_g, i_b, f_b, hn_c_g, w_branch_a, w_branch_b, w_branch_c, w_out, norm2_g, w_ff1, w_ff2):
    depth = w_in.shape[0]
    nbp, seq, d = x_prompt.shape
    nbs, t_new, _ = x_sample.shape
    buf = cache_swa_k.shape[2]
    hd = B_HEADS * B_HEAD_DIM
    a_width = d // 2
    qk_w = C_HEADS * C_QK_DIM
    v_w = C_HEADS * C_V_DIM
    n_gate = N_BRANCH * d
    n_main = w_in.shape[2] - n_gate - IF_COLS
    col_b = 2 * a_width
    col_c = col_b + 3 * hd
    assert col_c + 2 * qk_w + 2 * v_w == n_main and seq % A_CHUNK == 0 and t_new <= A_CHUNK

    mp, ms = nbp * seq, nbs * t_new
    quantum = BF16_SUBLANES * math.lcm(ROW_TILES, MERGE_ROW_TILES)
    m_tot = -(-(mp + ms) // quantum) * quantum
    n_tail = m_tot - mp
    tm = m_tot // ROW_TILES
    tm_merge = m_tot // MERGE_ROW_TILES
    tm_norm = m_tot // (ROW_TILES * 3)
    assert mp % n_tail == 0 and n_tail % t_new == 0 and tm_norm % BF16_SUBLANES == 0
    x = jnp.concatenate([x_prompt.reshape(mp, d), x_sample.reshape(ms, d), jnp.zeros((m_tot - mp - ms, d), F32)])

    w_in_t = jnp.swapaxes(w_in, 1, 2)
    zero_state = (jnp.zeros((nbp, C_HEADS, C_V_DIM, C_QK_DIM), F32), jnp.zeros((nbp, C_HEADS, C_QK_DIM), F32),
                  jnp.zeros((nbp, C_HEADS, LANES), F32))
    lane = jnp.arange(LANES)
    gate_pad_row = jnp.where(lane < C_HEADS, -GATE_PAD, jnp.where(lane < IF_COLS, GATE_PAD, 0.0)).astype(F32)
    seqs_tail = n_tail // t_new

    kv_prompt = None
    small = [[] for _ in range(9)]
    for l in range(depth):
        w_if = jnp.pad(w_in_t[l, n_main:n_main + IF_COLS, :], ((0, LANES - IF_COLS), (0, 0)))
        bias = jnp.concatenate([i_b[l], f_b[l]])
        bias_c = jnp.pad(bias, (0, LANES - IF_COLS)).reshape(1, LANES)
        bias_r = bias.reshape(IF_COLS, 1)

        xn = _rmsnorm_bf16(x, norm1_g[l], tm=tm_norm)
        proj = _matmul(xn, w_in_t, layer=l, n_cols=n_main, tm=tm, w_transposed=True)
        ifp = _matmul(xn, w_if, tm=tm, w_transposed=True)
        gates = _matmul(xn, w_in_t, layer=l, n_cols=n_gate, col_off=n_main, row_shift=IF_COLS, tm=tm,
                        w_transposed=True)

        a_out, _ = _gmlp(proj, ws_a[l], bs_a[l].T, norm_va_g[l], width=a_width, chunk=A_CHUNK, chunks=2,
                         row0=0, n_rows=mp, out_rows=m_tot, want_va=False)
        eye = jnp.eye(seqs_tail, dtype=F32)
        ws_tail = jax.vmap(lambda w: jnp.kron(eye, w[:t_new, :t_new]))(ws_a[l])
        bs_tail = jnp.tile(bs_a[l][:, :t_new], (1, seqs_tail)).T
        a_tail, va_tail = _gmlp(proj, ws_tail, bs_tail, norm_va_g[l], width=a_width, chunk=n_tail, chunks=1,
                                row0=mp, n_rows=n_tail, out_rows=n_tail, want_va=True)
        a_out = lax.dynamic_update_slice(a_out, a_tail, (mp, 0))

        b_out, kp, vp = _attn_prompt(proj, qn_g[l], kn_g[l], n_seq=nbp, seq=seq, col0=col_b // B_HEAD_DIM,
                                     out_rows=m_tot, layer=l, depth=depth, kv_prev=kv_prompt)
        kv_prompt = (kp, vp)
        qkv_s = proj[mp:mp + ms, col_b:col_c].reshape(nbs, t_new, 3 * hd)
        b_s, k_s, v_s = _attn_sample(qkv_s, cache_swa_k, cache_swa_v, qn_g[l], kn_g[l], layer=l)
        b_out = _put_rows(b_out, b_s.reshape(ms, hd), mp, n_tail)

        cols = dict(col_q=col_c, col_k=col_c + qk_w, col_v=col_c + 2 * qk_w, col_o=col_c + 2 * qk_w + v_w)
        ifr = jnp.swapaxes(ifp[:mp, :IF_COLS].reshape(nbp, seq, IF_COLS), 1, 2)
        c_out, cp, n_p, m_p = _mlstm(proj, ifp, ifr, bias_c, bias_r, hn_c_g[l], *zero_state, n_seq=nbp, seq=seq,
                                     ln=MLSTM_CHUNK, out_rows=m_tot, **cols)
        pad = A_CHUNK - t_new
        proj_s = jnp.pad(proj[mp:mp + ms, col_c:].reshape(nbs, t_new, n_main - col_c), ((0, 0), (0, pad), (0, 0)))
        ifc_s = jnp.concatenate([ifp[mp:mp + ms].reshape(nbs, t_new, LANES),
                                 jnp.broadcast_to(gate_pad_row, (nbs, pad, LANES))], axis=1)
        ifr_s = jnp.swapaxes(ifc_s[:, :, :IF_COLS], 1, 2)
        m0 = jnp.broadcast_to(state_mlstm_m[l][..., None], (nbs, C_HEADS, LANES))
        c_s_out, c_s, n_s, m_s = _mlstm(
            proj_s.reshape(nbs * A_CHUNK, -1), ifc_s.reshape(nbs * A_CHUNK, LANES), ifr_s, bias_c, bias_r, hn_c_g[l],
            state_mlstm_C[l], state_mlstm_n[l], m0, n_seq=nbs, seq=A_CHUNK, ln=A_CHUNK, out_rows=nbs * A_CHUNK,
            col_q=0, col_k=qk_w, col_v=2 * qk_w, col_o=2 * qk_w + v_w)
        c_out = _put_rows(c_out, c_s_out.reshape(nbs, A_CHUNK, v_w)[:, :t_new].reshape(ms, v_w), mp, n_tail)

        merged = _merge(a_out, b_out, c_out, w_branch_a, w_branch_b, w_branch_c, gates, layer=l, tm=tm_merge)
        x = _matmul(merged, w_out, layer=l, epilogue="resid", resid=x, tm=tm, vmem_limit_bytes=WIDE_VMEM_LIMIT_BYTES)
        h = _matmul(_rmsnorm_bf16(x, norm2_g[l], tm=tm_norm), w_ff1, layer=l, epilogue="relu2", out_dtype=BF16, tm=tm)
        k_len = w_ff2.shape[1] // FF2_SPLITS
        for part in range(FF2_SPLITS):
            final = l == depth - 1 and part == FF2_SPLITS - 1
            x = _matmul(h, w_ff2, layer=l, k_off=part * k_len, k_len=k_len, epilogue="resid", resid=x,
                        tm=tm, main_rows=mp if final else None, vmem_limit_bytes=WIDE_VMEM_LIMIT_BYTES)

        vals = (cp, n_p, m_p[..., 0], k_s.reshape(nbs, t_new, B_HEADS, B_HEAD_DIM),
                v_s.reshape(nbs, t_new, B_HEADS, B_HEAD_DIM), c_s, n_s, m_s[..., 0],
                va_tail[:ms].reshape(nbs, t_new, a_width))
        for o, v in zip(small, vals):
            o.append(v)

    keep = min(max(w for w, _ in B_PATTERNS), seq)
    kp, vp = (t[:, :, seq - keep:].reshape(depth, nbp, keep, B_HEADS, B_HEAD_DIM) for t in kv_prompt)
    st = [jnp.stack(o) for o in small]
    y_prompt, y_tail = x
    return (y_prompt.reshape(nbp, seq, d), y_tail[:ms].reshape(nbs, t_new, d), kp, vp, st[0], st[1], st[2],
            st[3], st[4], st[5], st[6], st[7], st[8])
```
